```python
import math
import jax, jax.numpy as jnp
from jax import lax
import numpy as np

D_MODEL = 1024
BATCH = 4
SEQ = 8192
DEPTH = 1

CTX_LEN = 256
GRID_W = 64
MIX_WIDTH = D_MODEL
LRU_WIDTH = MIX_WIDTH // 2
LRU_BLOCKS = 8
LRU_BLOCK = LRU_WIDTH // LRU_BLOCKS
CONV_W = 4
LRU_C = 8.0
ATT_WIDTH = MIX_WIDTH - LRU_WIDTH
ATT_HEADS = 4
ATT_V_DIM = ATT_WIDTH // ATT_HEADS
ATT_HEAD_DIM = ATT_V_DIM // 2
QK_WIDTH = ATT_HEADS * 2 * ATT_HEAD_DIM
IN_WIDTH = 2 * LRU_WIDTH + 2 * QK_WIDTH + ATT_WIDTH
ROPE_FREQS = ATT_HEAD_DIM // 4
ROPE_BASE = 10000.0
Q_BLOCK = 128
N_GROUPS = 4
EXPERTS_PER_GROUP = 8
N_EXPERTS = N_GROUPS * EXPERTS_PER_GROUP
TOP_K = 2
D_EXPERT = D_MODEL // 2
LN_EPS = 1e-5
DEEPNORM_ALPHA = (2.0 * DEPTH) ** 0.25
DEEPNORM_BETA = (8.0 * DEPTH) ** -0.25

kernel_name = 'hymba_rglru_diffattn_hiermoe_dit_block'


def _ln(x):
    xf = x.astype(jnp.float32)
    mu = jnp.mean(xf, -1, keepdims=True)
    var = jnp.mean(jnp.square(xf - mu), -1, keepdims=True)
    return ((xf - mu) * lax.rsqrt(var + LN_EPS)).astype(x.dtype)


def _post_ln(x, g, b):
    return _ln(x) * g + b


def _modulate(x, shift, scale):
    return _ln(x) * (1 + scale) + shift


def _adaln(cond, w, b):
    m = (jax.nn.silu(cond) @ w + b).reshape(cond.shape[0], 6, 1, D_MODEL)
    return m[:, 0], m[:, 1], m[:, 2], m[:, 3], m[:, 4], m[:, 5]


def _split_in(p):
    return jnp.split(p, [LRU_WIDTH, 2 * LRU_WIDTH, 2 * LRU_WIDTH + QK_WIDTH,
                         2 * LRU_WIDTH + 2 * QK_WIDTH], axis=-1)


def _centred_conv(x, w, b):
    L = x.shape[1]
    left = CONV_W // 2
    xp = jnp.pad(x, ((0, 0), (left, CONV_W - 1 - left), (0, 0)))
    out = b
    for tap in range(CONV_W):
        out = out + w[tap] * xp[:, tap:tap + L]
    return out


def _rglru_coeffs(xc, w_a, b_a, w_i, b_i, lam):
    B, L, W = xc.shape
    xb = xc.reshape(B, L, LRU_BLOCKS, LRU_BLOCK)
    r = jax.nn.sigmoid(jnp.einsum('blnk,nkj->blnj', xb, w_a).reshape(B, L, W) + b_a)
    i = jax.nn.sigmoid(jnp.einsum('blnk,nkj->blnj', xb, w_i).reshape(B, L, W) + b_i)
    log_a = -LRU_C * r.astype(jnp.float32) * jax.nn.softplus(-lam.astype(jnp.float32))
    a = jnp.exp(log_a)
    mult = jnp.sqrt(-jnp.expm1(2.0 * log_a))
    return a, mult * (i * xc).astype(jnp.float32)


def _linear_scan(a, b, h0, reverse):
    if reverse:
        b = b.at[:, -1].add(a[:, -1] * h0)
    else:
        b = b.at[:, 0].add(a[:, 0] * h0)

    def combine(e1, e2):
        a1, b1 = e1
        a2, b2 = e2
        return a1 * a2, a2 * b1 + b2

    _, h = lax.associative_scan(combine, (a, b), axis=1, reverse=reverse)
    return h


def _axial_rope(n_tokens):
    rows = n_tokens // GRID_W
    row = jnp.repeat(jnp.arange(rows, dtype=jnp.float32), GRID_W)
    col = jnp.tile(jnp.arange(GRID_W, dtype=jnp.float32), rows)
    inv = ROPE_BASE ** (-jnp.arange(ROPE_FREQS, dtype=jnp.float32) / ROPE_FREQS)
    ang = jnp.stack([row[:, None] * inv, col[:, None] * inv], axis=1)
    ang = ang[:, None, None]
    return jnp.cos(ang), jnp.sin(ang)


def _rope(t, cos, sin):
    B, L, H, M, d = t.shape
    tr = t.reshape(B, L, H, M, 2, 2, ROPE_FREQS)
    t1, t2 = tr[..., 0, :], tr[..., 1, :]
    out = jnp.stack([t1 * cos - t2 * sin, t2 * cos + t1 * sin], axis=-2)
    return out.reshape(B, L, H, M, d).astype(t.dtype)


def _heads_qk(t):
    return t.reshape(t.shape[0], t.shape[1], ATT_HEADS, 2, ATT_HEAD_DIM)


def _heads_v(t):
    return t.reshape(t.shape[0], t.shape[1], ATT_HEADS, ATT_V_DIM)


def _diff_attend(q, k, v, lam):
    s = jnp.einsum('bqhmd,bkhmd->bhmqk', q, k).astype(jnp.float32) * (ATT_HEAD_DIM ** -0.5)
    p = jax.nn.softmax(s, axis=-1)
    a = p[:, :, 0] - lam * p[:, :, 1]
    return jnp.einsum('bhqk,bkhv->bqhv', a.astype(v.dtype), v)


def _diff_attention_latent(q, k, v, k_c, v_c, lam):
    B, L = q.shape[:2]
    k_all = jnp.concatenate([k, k_c], axis=1)
    v_all = jnp.concatenate([v, v_c], axis=1)
    qb = q.reshape(B, L // Q_BLOCK, Q_BLOCK, ATT_HEADS, 2, ATT_HEAD_DIM).swapaxes(0, 1)
    o = lax.map(lambda blk: _diff_attend(blk, k_all, v_all, lam), qb)
    return o.swapaxes(0, 1).reshape(B, L, ATT_HEADS, ATT_V_DIM)


def _head_rmsnorm(o, g, lam_init):
    of = o.astype(jnp.float32)
    of = of * lax.rsqrt(jnp.mean(jnp.square(of), -1, keepdims=True) + LN_EPS) * (1.0 - lam_init)
    out = of.astype(o.dtype) * g
    return out.reshape(o.shape[0], o.shape[1], ATT_WIDTH)


def _diff_lambda(lp, lam_init):
    lpf = lp.astype(jnp.float32)
    return jnp.exp(jnp.sum(lpf[0] * lpf[1])) - jnp.exp(jnp.sum(lpf[2] * lpf[3])) + lam_init


def _hier_moe(h, wg, bg, we, be, w1, w3, w2):
    gp = jax.nn.softmax((h @ wg + bg).astype(jnp.float32), axis=-1)
    g_sel = jnp.argmax(gp, axis=-1)
    p_g = jnp.max(gp, axis=-1)
    el = jnp.einsum('bld,gde->blge', h, we) + be
    el_sel = jnp.einsum('blge,blg->ble', el.astype(jnp.float32),
                        jax.nn.one_hot(g_sel, N_GROUPS, dtype=jnp.float32))
    ep = jax.nn.softmax(el_sel, axis=-1)
    top_w, top_i = lax.top_k(ep, TOP_K)
    top_w = top_w / jnp.sum(top_w, -1, keepdims=True)
    expert_id = g_sel[..., None] * EXPERTS_PER_GROUP + top_i
    comb = jnp.sum(jax.nn.one_hot(expert_id, N_EXPERTS, dtype=jnp.float32)
                   * (p_g[..., None] * top_w)[..., None], axis=-2).astype(h.dtype)
    y = jnp.zeros_like(h)
    for e in range(N_EXPERTS):
        act = jax.nn.silu(h @ w1[e]) * (h @ w3[e])
        y = y + comb[..., e:e + 1] * (act @ w2[e])
    return y


def setup_inputs(seed: int = 0) -> dict:
    key = jax.random.key(seed)
    ks = jax.random.split(key, 28)
    f32 = jnp.float32
    D = D_MODEL

    def nrm(k, shape, s):
        return s * jax.random.normal(k, shape, f32)

    x = nrm(ks[0], (BATCH, SEQ, D), 1.0)
    c = nrm(ks[1], (BATCH, D), 1.0)
    ctx = nrm(ks[2], (BATCH, CTX_LEN, D), 1.0)
    c_ctx = nrm(ks[3], (D,), 1.0)
    w_mod = nrm(ks[4], (DEPTH, D, 6 * D), 0.1 * D ** -0.5)
    gate_offset = jnp.repeat(jnp.array([0.0, 0.0, 1.0, 0.0, 0.0, 1.0], f32), D)
    b_mod = nrm(ks[5], (DEPTH, 6 * D), 0.05) + gate_offset
    w_in = nrm(ks[6], (DEPTH, D, IN_WIDTH), D ** -0.5)
    w_in = w_in.at[..., IN_WIDTH - ATT_WIDTH:].multiply(DEEPNORM_BETA)
    conv_w = nrm(ks[7], (DEPTH, CONV_W, LRU_WIDTH), CONV_W ** -0.5)
    conv_b = nrm(ks[8], (DEPTH, LRU_WIDTH), 0.02)
    lru_wa = nrm(ks[9], (DEPTH, 2, LRU_BLOCKS, LRU_BLOCK, LRU_BLOCK), LRU_BLOCK ** -0.5)
    lru_ba = nrm(ks[10], (DEPTH, 2, LRU_WIDTH), 0.02)
    lru_wi = nrm(ks[11], (DEPTH, 2, LRU_BLOCKS, LRU_BLOCK, LRU_BLOCK), LRU_BLOCK ** -0.5)
    lru_bi = nrm(ks[12], (DEPTH, 2, LRU_WIDTH), 0.02)
    a0 = jax.random.uniform(ks[13], (DEPTH, 2, LRU_WIDTH), f32, 0.9, 0.999)
    lru_lambda = jnp.log(a0) - jnp.log1p(-a0)
    diff_lambda = nrm(ks[14], (DEPTH, 4, ATT_HEAD_DIM), 0.1)
    attn_norm_g = 1.0 + nrm(ks[15], (DEPTH, ATT_HEADS, ATT_V_DIM), 0.02)
    w_out = nrm(ks[16], (DEPTH, MIX_WIDTH, D), MIX_WIDTH ** -0.5 * DEEPNORM_BETA)
    post_g = 1.0 + nrm(ks[17], (DEPTH, 2, D), 0.02)
    post_b = nrm(ks[18], (DEPTH, 2, D), 0.02)
    router_g_w = nrm(ks[19], (DEPTH, D, N_GROUPS), D ** -0.5)
    router_g_b = nrm(ks[20], (DEPTH, N_GROUPS), 0.01)
    router_e_w = nrm(ks[21], (DEPTH, N_GROUPS, D, EXPERTS_PER_GROUP), D ** -0.5)
    router_e_b = nrm(ks[22], (DEPTH, N_GROUPS, EXPERTS_PER_GROUP), 0.01)
    exp_w1 = nrm(ks[23], (DEPTH, N_EXPERTS, D, D_EXPERT), D ** -0.5 * DEEPNORM_BETA)
    exp_w3 = nrm(ks[24], (DEPTH, N_EXPERTS, D, D_EXPERT), D ** -0.5 * DEEPNORM_BETA)
    exp_w2 = nrm(ks[25], (DEPTH, N_EXPERTS, D_EXPERT, D), D_EXPERT ** -0.5 * DEEPNORM_BETA)
    return {'x': x, 'c': c, 'ctx': ctx, 'c_ctx': c_ctx, 'w_mod': w_mod, 'b_mod': b_mod,
            'w_in': w_in, 'conv_w': conv_w, 'conv_b': conv_b, 'lru_wa': lru_wa, 'lru_ba': lru_ba,
            'lru_wi': lru_wi, 'lru_bi': lru_bi, 'lru_lambda': lru_lambda, 'diff_lambda': diff_lambda,
            'attn_norm_g': attn_norm_g, 'w_out': w_out, 'post_g': post_g, 'post_b': post_b,
            'router_g_w': router_g_w, 'router_g_b': router_g_b, 'router_e_w': router_e_w,
            'router_e_b': router_e_b, 'exp_w1': exp_w1, 'exp_w3': exp_w3, 'exp_w2': exp_w2}


def reference(x, c, ctx, c_ctx, w_mod, b_mod, w_in, conv_w, conv_b, lru_wa, lru_ba, lru_wi, lru_bi,
              lru_lambda, diff_lambda, attn_norm_g, w_out, post_g, post_b, router_g_w, router_g_b,
              router_e_w, router_e_b, exp_w1, exp_w3, exp_w2):
    cos, sin = _axial_rope(x.shape[1])
    for l in range(DEPTH):
        lam_init = 0.8 - 0.6 * math.exp(-0.3 * l)
        sh1, sc1, g1, sh2, sc2, g2 = _adaln(c, w_mod[l], b_mod[l])
        sh1c, sc1c, g1c, sh2c, sc2c, g2c = _adaln(c_ctx[None], w_mod[l], b_mod[l])

        u, gt, q, k, v = _split_in(_modulate(x, sh1, sc1) @ w_in[l])
        u_c, gt_c, q_c, k_c, v_c = _split_in(_modulate(ctx, sh1c, sc1c) @ w_in[l])

        xc = _centred_conv(u, conv_w[l], conv_b[l])
        xc_c = _centred_conv(u_c, conv_w[l], conv_b[l])
        rec = jnp.zeros(xc.shape, jnp.float32)
        ctx_states = []
        for d in range(2):
            rev = d == 1
            a_c, b_c = _rglru_coeffs(xc_c, lru_wa[l, d], lru_ba[l, d], lru_wi[l, d], lru_bi[l, d], lru_lambda[l, d])
            h_c = _linear_scan(a_c, b_c, jnp.zeros_like(b_c[:, 0]), rev)
            ctx_states.append(h_c)
            h0 = h_c[:, 0] if rev else h_c[:, -1]
            a_x, b_x = _rglru_coeffs(xc, lru_wa[l, d], lru_ba[l, d], lru_wi[l, d], lru_bi[l, d], lru_lambda[l, d])
            rec = rec + _linear_scan(a_x, b_x, h0, rev)
        y_lru = jax.nn.gelu(gt) * rec.astype(x.dtype)

        lam = _diff_lambda(diff_lambda[l], lam_init)
        q_h = _rope(_heads_qk(q), cos, sin)
        k_h = _rope(_heads_qk(k), cos, sin)
        k_ch = _heads_qk(k_c)
        v_ch = _heads_v(v_c)
        o = _diff_attention_latent(q_h, k_h, _heads_v(v), k_ch, v_ch, lam)
        y_att = _head_rmsnorm(o, attn_norm_g[l], lam_init)

        mix = jnp.concatenate([y_lru, y_att], axis=-1) @ w_out[l]
        x_mid = _post_ln(DEEPNORM_ALPHA * x + g1 * mix, post_g[l, 0], post_b[l, 0])
        y_ffn = _hier_moe(_modulate(x_mid, sh2, sc2), router_g_w[l], router_g_b[l], router_e_w[l],
                          router_e_b[l], exp_w1[l], exp_w3[l], exp_w2[l])
        x_new = _post_ln(DEEPNORM_ALPHA * x_mid + g2 * y_ffn, post_g[l, 1], post_b[l, 1])

        if l + 1 < DEPTH:
            y_lru_c = jax.nn.gelu(gt_c) * (ctx_states[0] + ctx_states[1]).astype(ctx.dtype)
            o_c = _diff_attend(_heads_qk(q_c), k_ch, v_ch, lam)
            y_att_c = _head_rmsnorm(o_c, attn_norm_g[l], lam_init)
            mix_c = jnp.concatenate([y_lru_c, y_att_c], axis=-1) @ w_out[l]
            ctx_mid = _post_ln(DEEPNORM_ALPHA * ctx + g1c * mix_c, post_g[l, 0], post_b[l, 0])
            y_ffn_c = _hier_moe(_modulate(ctx_mid, sh2c, sc2c), router_g_w[l], router_g_b[l], router_e_w[l],
                                router_e_b[l], exp_w1[l], exp_w3[l], exp_w2[l])
            ctx = _post_ln(DEEPNORM_ALPHA * ctx_mid + g2c * y_ffn_c, post_g[l, 1], post_b[l, 1])
        x = x_new
    return x
```

```python
import functools
import math

import jax
import jax.numpy as jnp
from jax import lax
from jax.experimental import pallas as pl
from jax.experimental.pallas import tpu as pltpu

F32 = jnp.float32
BF16 = jnp.bfloat16

LN_EPS = 1e-5
LRU_C = 8.0
ROPE_BASE = 10000.0
GRID_W = 64
HEAD_V_DIM = 128
HEAD_QK_DIM = HEAD_V_DIM // 2
CONV_W = 4
SUBLANES = 8
LANES = 128
NEG_BIG = -1e30
VMEM_LIMIT_BYTES = 56 * 1024 * 1024


def _cparams(*sem):
    return pltpu.CompilerParams(dimension_semantics=sem, vmem_limit_bytes=VMEM_LIMIT_BYTES)


def _layer_norm(x):
    mu = jnp.mean(x, axis=-1, keepdims=True)
    xc = x - mu
    var = jnp.mean(xc * xc, axis=-1, keepdims=True)
    return xc * lax.rsqrt(var + LN_EPS)


def _split_bf16(x):
    hi = x.astype(BF16)
    lo = (x - hi.astype(F32)).astype(BF16)
    return hi, lo


def _dot(a, b):
    return jnp.dot(a, b, preferred_element_type=F32)


def _dot_split(a_hi, a_lo, b_hi, b_lo):
    return _dot(a_hi, b_hi) + _dot(a_hi, b_lo) + _dot(a_lo, b_hi)


def _mods_kernel(c_ref, w_ref, b_ref, o_ref):
    c = c_ref[...]
    s = c * jax.nn.sigmoid(c)
    s_hi, s_lo = _split_bf16(s)
    w_hi, w_lo = _split_bf16(w_ref[...])
    o_ref[...] = _dot_split(s_hi, s_lo, w_hi, w_lo) + b_ref[...]


def _mods(cond, w, b):
    rows, d = cond.shape
    n = w.shape[1]
    tn = min(n, 1024)
    return pl.pallas_call(
        _mods_kernel,
        out_shape=jax.ShapeDtypeStruct((rows, n), F32),
        grid=(n // tn,),
        in_specs=[pl.BlockSpec((rows, d), lambda j: (0, 0)),
                  pl.BlockSpec((d, tn), lambda j: (0, j)),
                  pl.BlockSpec((1, tn), lambda j: (0, j))],
        out_specs=pl.BlockSpec((rows, tn), lambda j: (0, j)),
        compiler_params=_cparams("arbitrary"),
        name="mods",
    )(cond, w, b.reshape(1, n))


def _rope_tables(n_tokens, grid_w):
    n_freq = HEAD_QK_DIM // 4
    pos = jnp.arange(n_tokens, dtype=jnp.int32)
    row = (pos // grid_w).astype(F32)
    col = (pos % grid_w).astype(F32)
    inv = ROPE_BASE ** (-jnp.arange(n_freq, dtype=F32) / n_freq)
    ar = row[:, None] * inv
    ac = col[:, None] * inv
    cos64 = jnp.concatenate([jnp.cos(ar), jnp.cos(ar), jnp.cos(ac), jnp.cos(ac)], axis=1)
    sin64 = jnp.concatenate([-jnp.sin(ar), jnp.sin(ar), -jnp.sin(ac), jnp.sin(ac)], axis=1)
    reps = HEAD_V_DIM // HEAD_QK_DIM
    return jnp.tile(cos64, (1, reps)), jnp.tile(sin64, (1, reps))


def _inproj_kernel(x_ref, sh_ref, sc_ref, w_ref, cos_ref, sin_ref,
                   u_ref, g_ref, q_ref, k_ref, v_ref, *, lru_w, n_heads, rope):
    h = _layer_norm(x_ref[0]) * (1.0 + sc_ref[0]) + sh_ref[0]
    p = _dot(h.astype(BF16), w_ref[...])
    u_ref[0] = p[:, :lru_w]
    g_ref[0] = p[:, lru_w:2 * lru_w].astype(BF16)
    qk_w = n_heads * HEAD_V_DIM
    base_q = 2 * lru_w
    base_k = base_q + qk_w
    base_v = base_k + qk_w
    if rope:
        cos = cos_ref[...]
        sin = sin_ref[...]
        lane = lax.broadcasted_iota(jnp.int32, cos.shape, 1)
        first = (lane % 32) < 16
    q_scale = HEAD_QK_DIM ** -0.5
    for hd in range(n_heads):
        lo, hi = hd * HEAD_V_DIM, (hd + 1) * HEAD_V_DIM
        for base, ref, scale in ((base_q, q_ref, q_scale), (base_k, k_ref, None)):
            t = p[:, base + lo:base + hi]
            if rope:
                partner = jnp.where(first, pltpu.roll(t, LANES - 16, 1), pltpu.roll(t, 16, 1))
                t = t * cos + partner * sin
            if scale is not None:
                t = t * scale
            ref[0, hd] = t.astype(BF16)
        v_ref[0, hd] = p[:, base_v + lo:base_v + hi].astype(BF16)


def _inproj(x, shift, scale, w_in, cos, sin, *, lru_w, n_heads, rope):
    bsz, seq, d = x.shape
    in_w = w_in.shape[1]
    tm = min(seq, 512)
    kern = functools.partial(_inproj_kernel, lru_w=lru_w, n_heads=n_heads, rope=rope)
    head_shape = jax.ShapeDtypeStruct((bsz, n_heads, seq, HEAD_V_DIM), BF16)
    head_spec = pl.BlockSpec((1, n_heads, tm, HEAD_V_DIM), lambda b, i: (b, 0, i, 0))
    vec_spec = pl.BlockSpec((1, 1, d), lambda b, i: (b, 0, 0))
    return pl.pallas_call(
        kern,
        out_shape=(jax.ShapeDtypeStruct((bsz, seq, lru_w), F32),
                   jax.ShapeDtypeStruct((bsz, seq, lru_w), BF16),
                   head_shape, head_shape, head_shape),
        grid=(bsz, seq // tm),
        in_specs=[pl.BlockSpec((1, tm, d), lambda b, i: (b, i, 0)),
                  vec_spec, vec_spec,
                  pl.BlockSpec((d, in_w), lambda b, i: (0, 0)),
                  pl.BlockSpec((tm, HEAD_V_DIM), lambda b, i: (i, 0)),
                  pl.BlockSpec((tm, HEAD_V_DIM), lambda b, i: (i, 0))],
        out_specs=(pl.BlockSpec((1, tm, lru_w), lambda b, i: (b, i, 0)),
                   pl.BlockSpec((1, tm, lru_w), lambda b, i: (b, i, 0)),
                   head_spec, head_spec, head_spec),
        compiler_params=_cparams("parallel", "arbitrary"),
        name="inproj_rope" if rope else "inproj_ctx",
    )(x, shift, scale, w_in, cos, sin)


def _scan_rows(a, b, reverse):
    row = lax.broadcasted_iota(jnp.int32, a.shape, 0)
    for s in (1, 2, 4):
        shift = SUBLANES - s if reverse else s
        a_sh = pltpu.roll(a, shift, 0)
        b_sh = pltpu.roll(b, shift, 0)
        valid = (row < SUBLANES - s) if reverse else (row >= s)
        b = jnp.where(valid, a * b_sh + b, b)
        a = jnp.where(valid, a * a_sh, a)
    return a, b


def _lru_kernel(*refs, reverse, combine, n_chunks, chunk):
    if combine:
        (u_ref, up_ref, un_ref, cw_ref, cb_ref, wg_ref, bg_ref, lam_ref, h0_ref, hb_ref, g_ref,
         out_ref, hn_ref, carry_ref, a_scr, b_scr) = refs
    else:
        (u_ref, up_ref, un_ref, cw_ref, cb_ref, wg_ref, bg_ref, lam_ref, h0_ref,
         out_ref, hn_ref, carry_ref, a_scr, b_scr) = refs
    c = pl.program_id(1)
    cc = (n_chunks - 1 - c) if reverse else c
    width = u_ref.shape[-1]

    @pl.when(c == 0)
    def _():
        carry_ref[...] = h0_ref[0]

    u = u_ref[0]
    prev = jnp.where(cc > 0, up_ref[0], 0.0)
    nxt = jnp.where(cc < n_chunks - 1, un_ref[0], 0.0)
    row = lax.broadcasted_iota(jnp.int32, u.shape, 0)
    u_m1 = jnp.where(row == 0, prev[7:8], pltpu.roll(u, 1, 0))
    u_m2 = jnp.where(row == 0, prev[6:7], jnp.where(row == 1, prev[7:8], pltpu.roll(u, 2, 0)))
    u_p1 = jnp.where(row == chunk - 1, nxt[0:1], pltpu.roll(u, chunk - 1, 0))
    cw = cw_ref[...]
    xc = cb_ref[...] + cw[0:1] * u_m2 + cw[1:2] * u_m1 + cw[2:3] * u + cw[3:4] * u_p1

    z = _dot(xc.astype(BF16), wg_ref[...]) + bg_ref[...]
    r = jax.nn.sigmoid(z[:, :width])
    i = jax.nn.sigmoid(z[:, width:])
    nlam = -lam_ref[...]
    softplus = jnp.maximum(nlam, 0.0) + jnp.log(1.0 + jnp.exp(-jnp.abs(nlam)))
    log_a = -LRU_C * r * softplus
    a = jnp.exp(log_a)
    a_scr[...] = a
    b_scr[...] = jnp.sqrt(1.0 - jnp.exp(2.0 * log_a)) * (i * xc)

    n_groups = chunk // SUBLANES

    def body(j, carry):
        jj = (n_groups - 1 - j) if reverse else j
        r0 = pl.multiple_of(jj * SUBLANES, SUBLANES)
        a_cum, h_loc = _scan_rows(a_scr[pl.ds(r0, SUBLANES), :], b_scr[pl.ds(r0, SUBLANES), :], reverse)
        h = a_cum * carry + h_loc
        b_scr[pl.ds(r0, SUBLANES), :] = h
        return h[0:1] if reverse else h[SUBLANES - 1:SUBLANES]

    carry = lax.fori_loop(0, n_groups, body, carry_ref[...])
    carry_ref[...] = carry
    hn_ref[0] = carry
    h_all = b_scr[...]
    if combine:
        out_ref[0] = (jax.nn.gelu(g_ref[0].astype(F32)) * (h_all + hb_ref[0])).astype(out_ref.dtype)
    else:
        out_ref[0] = h_all


def _lru(u, conv_w, conv_b, w_gate, b_gate, lam, h0, *, reverse, other=None, gate=None):
    bsz, seq, width = u.shape
    chunk = min(seq, 512)
    n_chunks = seq // chunk
    halo_per_chunk = chunk // SUBLANES
    n_halo = seq // SUBLANES
    combine = other is not None

    def pos(c):
        return (n_chunks - 1 - c) if reverse else c

    tile = lambda b, c: (b, pos(c), 0)
    prev_halo = lambda b, c: (b, jnp.maximum(pos(c) * halo_per_chunk - 1, 0), 0)
    next_halo = lambda b, c: (b, jnp.minimum((pos(c) + 1) * halo_per_chunk, n_halo - 1), 0)
    const2 = lambda b, c: (0, 0)
    state = lambda b, c: (b, 0, 0)
    in_specs = [pl.BlockSpec((1, chunk, width), tile),
                pl.BlockSpec((1, SUBLANES, width), prev_halo),
                pl.BlockSpec((1, SUBLANES, width), next_halo),
                pl.BlockSpec((CONV_W, width), const2),
                pl.BlockSpec((1, width), const2),
                pl.BlockSpec((width, 2 * width), const2),
                pl.BlockSpec((1, 2 * width), const2),
                pl.BlockSpec((1, width), const2),
                pl.BlockSpec((1, 1, width), state)]
    args = [u, u, u, conv_w, conv_b, w_gate, b_gate, lam, h0]
    if combine:
        in_specs += [pl.BlockSpec((1, chunk, width), tile), pl.BlockSpec((1, chunk, width), tile)]
        args += [other, gate]
    kern = functools.partial(_lru_kernel, reverse=reverse, combine=combine,
                             n_chunks=n_chunks, chunk=chunk)
    return pl.pallas_call(
        kern,
        out_shape=(jax.ShapeDtypeStruct((bsz, seq, width), BF16 if combine else F32),
                   jax.ShapeDtypeStruct((bsz, 1, width), F32)),
        grid=(bsz, n_chunks),
        in_specs=in_specs,
        out_specs=(pl.BlockSpec((1, chunk, width), tile), pl.BlockSpec((1, 1, width), state)),
        scratch_shapes=[pltpu.VMEM((1, width), F32),
                        pltpu.VMEM((chunk, width), F32),
                        pltpu.VMEM((chunk, width), F32)],
        compiler_params=_cparams("parallel", "arbitrary"),
        name=("lru_bwd" if reverse else "lru_fwd") + ("_mix" if combine else ""),
    )(*args)


def _expand_block_diag(w):
    n, k, _ = w.shape
    eye = jnp.eye(n, dtype=w.dtype)
    return (eye[:, None, :, None] * w[:, :, None, :]).reshape(n * k, n * k)


def _attn_kernel(dl_ref, q_ref, k_ref, v_ref, kc_ref, vc_ref, gn_ref, o_ref,
                 m_scr, l_scr, acc_scr, *, tk, lam_init):
    tq = q_ref.shape[2]
    seq = k_ref.shape[2]
    q = q_ref[0, 0]
    lane = lax.broadcasted_iota(jnp.int32, q.shape, 1)
    zero = jnp.zeros_like(q)
    q2 = jnp.concatenate([jnp.where(lane < HEAD_QK_DIM, q, zero),
                          jnp.where(lane >= HEAD_QK_DIM, q, zero)], axis=0)
    m_scr[...] = jnp.full(m_scr.shape, -jnp.inf, F32)
    l_scr[...] = jnp.zeros(l_scr.shape, F32)
    acc_scr[...] = jnp.zeros(acc_scr.shape, F32)

    def step(kb, vb):
        s = lax.dot_general(q2, kb, (((1,), (1,)), ((), ())), preferred_element_type=F32)
        m_prev = m_scr[...]
        m_new = jnp.maximum(m_prev, jnp.max(s, axis=-1, keepdims=True))
        alpha = jnp.exp(m_prev - m_new)
        p = jnp.exp(s - m_new)
        l_scr[...] = alpha * l_scr[...] + jnp.sum(p, axis=-1, keepdims=True)
        acc_scr[...] = alpha * acc_scr[...] + _dot(p.astype(BF16), vb)
        m_scr[...] = m_new

    def body(j, carry):
        off = pl.multiple_of(j * tk, tk)
        step(k_ref[0, 0, pl.ds(off, tk), :], v_ref[0, 0, pl.ds(off, tk), :])
        return carry

    lax.fori_loop(0, seq // tk, body, 0)
    step(kc_ref[0, 0], vc_ref[0, 0])

    lp = dl_ref[...]
    lam = (jnp.exp(jnp.sum(lp[0:1] * lp[1:2], axis=-1, keepdims=True))
           - jnp.exp(jnp.sum(lp[2:3] * lp[3:4], axis=-1, keepdims=True)) + lam_init)
    o_all = acc_scr[...] / l_scr[...]
    o = o_all[:tq] - lam * o_all[tq:]
    y = o * lax.rsqrt(jnp.mean(o * o, axis=-1, keepdims=True) + LN_EPS) * (1.0 - lam_init)
    o_ref[0] = (y * gn_ref[0]).astype(o_ref.dtype)


def _attention(diff_lambda, q, k, v, k_c, v_c, norm_g, *, lam_init):
    bsz, n_heads, seq, dv = q.shape
    n_ctx = k_c.shape[2]
    tq = min(seq, 256)
    tk = min(seq, 1024)
    kern = functools.partial(_attn_kernel, tk=tk, lam_init=lam_init)
    kv_spec = pl.BlockSpec((1, 1, seq, dv), lambda b, h, i: (b, h, 0, 0))
    ctx_spec = pl.BlockSpec((1, 1, n_ctx, dv), lambda b, h, i: (b, h, 0, 0))
    return pl.pallas_call(
        kern,
        out_shape=jax.ShapeDtypeStruct((bsz, seq, n_heads * dv), BF16),
        grid=(bsz, n_heads, seq // tq),
        in_specs=[pl.BlockSpec(diff_lambda.shape, lambda b, h, i: (0, 0)),
                  pl.BlockSpec((1, 1, tq, dv), lambda b, h, i: (b, h, i, 0)),
                  kv_spec, kv_spec, ctx_spec, ctx_spec,
                  pl.BlockSpec((1, 1, dv), lambda b, h, i: (h, 0, 0))],
        out_specs=pl.BlockSpec((1, tq, dv), lambda b, h, i: (b, i, h)),
        scratch_shapes=[pltpu.VMEM((2 * tq, 1), F32),
                        pltpu.VMEM((2 * tq, 1), F32),
                        pltpu.VMEM((2 * tq, dv), F32)],
        compiler_params=_cparams("parallel", "parallel", "arbitrary"),
        name="diff_attn",
    )(diff_lambda, q, k, v, k_c, v_c, norm_g)


def _mid_kernel(x_ref, yl_ref, ya_ref, wo_ref, g1_ref, sh2_ref, sc2_ref, pg_ref, pb_ref,
                wrh_ref, wrl_ref, br_ref, xmid_ref, h2_ref, comb_ref,
                *, alpha, n_groups, per_group):
    lru_w = yl_ref.shape[-1]
    mix = _dot(yl_ref[0], wo_ref[:lru_w, :]) + _dot(ya_ref[0], wo_ref[lru_w:, :])
    x_mid = _layer_norm(alpha * x_ref[0] + g1_ref[0] * mix) * pg_ref[...] + pb_ref[...]
    xmid_ref[0] = x_mid
    h2 = _layer_norm(x_mid) * (1.0 + sc2_ref[0]) + sh2_ref[0]
    h2_ref[0] = h2.astype(BF16)

    h_hi, h_lo = _split_bf16(h2)
    logits = _dot_split(h_hi, h_lo, wrh_ref[...], wrl_ref[...]) + br_ref[...]
    lane = lax.broadcasted_iota(jnp.int32, logits.shape, 1)

    def first_argmax(vals, vmax):
        return jnp.min(jnp.where(vals == vmax, lane, LANES), axis=-1, keepdims=True)

    gl = jnp.where(lane < n_groups, logits, NEG_BIG)
    g_max = jnp.max(gl, axis=-1, keepdims=True)
    p_g = 1.0 / jnp.sum(jnp.exp(gl - g_max), axis=-1, keepdims=True)
    g_sel = first_argmax(gl, g_max)
    e_lo = n_groups + g_sel * per_group
    el = jnp.where((lane >= e_lo) & (lane < e_lo + per_group), logits, NEG_BIG)
    m1 = jnp.max(el, axis=-1, keepdims=True)
    i1 = first_argmax(el, m1)
    el2 = jnp.where(lane == i1, NEG_BIG, el)
    m2 = jnp.max(el2, axis=-1, keepdims=True)
    i2 = first_argmax(el2, m2)
    e2 = jnp.exp(m2 - m1)
    w1 = 1.0 / (1.0 + e2)
    w2 = e2 * w1
    comb_ref[0] = (jnp.where(lane == i1 - n_groups, p_g * w1, 0.0)
                   + jnp.where(lane == i2 - n_groups, p_g * w2, 0.0))


def _mid(x, y_lru, y_att, w_out, g1, sh2, sc2, post_g, post_b, wr_hi, wr_lo, b_r,
         *, alpha, n_groups, per_group):
    bsz, seq, d = x.shape
    lru_w = y_lru.shape[-1]
    att_w = y_att.shape[-1]
    tm = min(seq, 512)
    kern = functools.partial(_mid_kernel, alpha=alpha, n_groups=n_groups, per_group=per_group)
    tok = lambda w: pl.BlockSpec((1, tm, w), lambda b, i: (b, i, 0))
    vec = pl.BlockSpec((1, 1, d), lambda b, i: (b, 0, 0))
    const = lambda r, c: pl.BlockSpec((r, c), lambda b, i: (0, 0))
    return pl.pallas_call(
        kern,
        out_shape=(jax.ShapeDtypeStruct((bsz, seq, d), F32),
                   jax.ShapeDtypeStruct((bsz, seq, d), BF16),
                   jax.ShapeDtypeStruct((bsz, seq, LANES), F32)),
        grid=(bsz, seq // tm),
        in_specs=[tok(d), tok(lru_w), tok(att_w), const(lru_w + att_w, d), vec, vec, vec,
                  const(1, d), const(1, d), const(d, LANES), const(d, LANES), const(1, LANES)],
        out_specs=(tok(d), tok(d), tok(LANES)),
        compiler_params=_cparams("parallel", "arbitrary"),
        name="mid_router",
    )(x, y_lru, y_att, w_out, g1, sh2, sc2, post_g, post_b, wr_hi, wr_lo, b_r)


def _moe_kernel(h2_ref, comb_ref, w1_ref, w3_ref, w2_ref, xmid_ref, g2_ref, pg_ref, pb_ref,
                out_ref, acc_scr, *, alpha, n_experts):
    e = pl.program_id(2)

    @pl.when(e == 0)
    def _():
        acc_scr[...] = jnp.zeros(acc_scr.shape, F32)

    h = h2_ref[0]
    a = _dot(h, w1_ref[0])
    act = a * jax.nn.sigmoid(a) * _dot(h, w3_ref[0])
    y = _dot(act.astype(BF16), w2_ref[0])
    comb = comb_ref[0]
    lane = lax.broadcasted_iota(jnp.int32, comb.shape, 1)
    c_e = jnp.sum(jnp.where(lane == e, comb, 0.0), axis=-1, keepdims=True)
    acc_scr[...] += c_e * y

    @pl.when(e == n_experts - 1)
    def _():
        z = alpha * xmid_ref[0] + g2_ref[0] * acc_scr[...]
        out_ref[0] = _layer_norm(z) * pg_ref[...] + pb_ref[...]


def _moe(h2, comb, w1, w3, w2, x_mid, g2, post_g, post_b, *, alpha):
    bsz, seq, d = x_mid.shape
    n_experts, _, d_e = w1.shape
    tm = min(seq, 1024)
    kern = functools.partial(_moe_kernel, alpha=alpha, n_experts=n_experts)
    tok = lambda w: pl.BlockSpec((1, tm, w), lambda b, i, e: (b, i, 0))
    return pl.pallas_call(
        kern,
        out_shape=jax.ShapeDtypeStruct((bsz, seq, d), F32),
        grid=(bsz, seq // tm, n_experts),
        in_specs=[tok(d), tok(LANES),
                  pl.BlockSpec((1, d, d_e), lambda b, i, e: (e, 0, 0)),
                  pl.BlockSpec((1, d, d_e), lambda b, i, e: (e, 0, 0)),
                  pl.BlockSpec((1, d_e, d), lambda b, i, e: (e, 0, 0)),
                  tok(d),
                  pl.BlockSpec((1, 1, d), lambda b, i, e: (b, 0, 0)),
                  pl.BlockSpec((1, d), lambda b, i, e: (0, 0)),
                  pl.BlockSpec((1, d), lambda b, i, e: (0, 0))],
        out_specs=tok(d),
        scratch_shapes=[pltpu.VMEM((tm, d), F32)],
        compiler_params=_cparams("parallel", "parallel", "arbitrary"),
        name="moe_experts",
    )(h2, comb, w1, w3, w2, x_mid, g2, post_g, post_b)


def _block(x, c, ctx, c_ctx, w_mod, b_mod, w_in, conv_w, conv_b, lru_wa, lru_ba, lru_wi, lru_bi,
           lru_lambda, diff_lambda, attn_norm_g, w_out, post_g, post_b, router_g_w, router_g_b,
           router_e_w, router_e_b, exp_w1, exp_w3, exp_w2, *, grid_w):
    depth = w_mod.shape[0]
    assert depth == 1, "single-layer block only"
    bsz, seq, d = x.shape
    lru_w = conv_w.shape[-1]
    n_heads, dv = attn_norm_g.shape[1:]
    assert dv == HEAD_V_DIM and conv_w.shape[1] == CONV_W
    n_groups, _, per_group = router_e_w.shape[1:]
    assert n_groups + n_groups * per_group <= LANES
    alpha = (2.0 * depth) ** 0.25
    lam_init = 0.8 - 0.6 * math.exp(0.0)

    cond = jnp.concatenate([c, c_ctx[None], jnp.zeros((SUBLANES - 1 - bsz % SUBLANES, d), F32)], axis=0)
    mods = _mods(cond, w_mod[0], b_mod[0]).reshape(cond.shape[0], 6, 1, d)
    sh1, sc1, g1, sh2, sc2, g2 = (mods[:bsz, j] for j in range(6))
    sh1c, sc1c = (jnp.broadcast_to(mods[bsz:bsz + 1, j], (bsz, 1, d)) for j in range(2))

    w_in_b = w_in[0].astype(BF16)
    cos, sin = _rope_tables(seq, grid_w)
    u, gt, q, k, v = _inproj(x, sh1, sc1, w_in_b, cos, sin, lru_w=lru_w, n_heads=n_heads, rope=True)
    n_ctx = ctx.shape[1]
    u_c, _, _, k_c, v_c = _inproj(ctx, sh1c, sc1c, w_in_b, cos[:n_ctx], sin[:n_ctx],
                                  lru_w=lru_w, n_heads=n_heads, rope=False)

    zero_state = jnp.zeros((bsz, 1, lru_w), F32)
    cb = conv_b[0].reshape(1, lru_w)

    def gate_params(direction):
        w_gate = jnp.concatenate([_expand_block_diag(lru_wa[0, direction]),
                                  _expand_block_diag(lru_wi[0, direction])], axis=1).astype(BF16)
        b_gate = jnp.concatenate([lru_ba[0, direction], lru_bi[0, direction]]).reshape(1, 2 * lru_w)
        return w_gate, b_gate, lru_lambda[0, direction].reshape(1, lru_w)

    fwd_p, bwd_p = gate_params(0), gate_params(1)
    _, seed_f = _lru(u_c, conv_w[0], cb, *fwd_p, zero_state, reverse=False)
    _, seed_b = _lru(u_c, conv_w[0], cb, *bwd_p, zero_state, reverse=True)
    h_bwd, _ = _lru(u, conv_w[0], cb, *bwd_p, seed_b, reverse=True)
    y_lru, _ = _lru(u, conv_w[0], cb, *fwd_p, seed_f, reverse=False, other=h_bwd, gate=gt)

    y_att = _attention(diff_lambda[0], q, k, v, k_c, v_c, attn_norm_g[0].reshape(n_heads, 1, dv),
                       lam_init=lam_init)

    w_r = jnp.concatenate([router_g_w[0], jnp.moveaxis(router_e_w[0], 0, 1).reshape(d, -1)], axis=1)
    b_r = jnp.concatenate([router_g_b[0], router_e_b[0].reshape(-1)])
    pad = LANES - w_r.shape[1]
    w_r = jnp.pad(w_r, ((0, 0), (0, pad)))
    b_r = jnp.pad(b_r, (0, pad)).reshape(1, LANES)
    wr_hi, wr_lo = _split_bf16(w_r)

    x_mid, h2, comb = _mid(x, y_lru, y_att, w_out[0].astype(BF16), g1, sh2, sc2,
                           post_g[0, 0].reshape(1, d), post_b[0, 0].reshape(1, d),
                           wr_hi, wr_lo, b_r, alpha=alpha, n_groups=n_groups, per_group=per_group)
    return _moe(h2, comb, exp_w1[0].astype(BF16), exp_w3[0].astype(BF16), exp_w2[0].astype(BF16),
                x_mid, g2, post_g[0, 1].reshape(1, d), post_b[0, 1].reshape(1, d), alpha=alpha)


def kernel(x, c, ctx, c_ctx, w_mod, b_mod, w_in, conv_w, conv_b, lru_wa, lru_ba, lru_wi, lru_bi, lru_lambda, diff_lambda, attn_norm_g, w_out, post_g, post_b, router_g_w, router_g_b, router_e_w, router_e_b, exp_w1, exp_w3, exp_w2):
    return _block(x, c, ctx, c_ctx, w_mod, b_mod, w_in, conv_w, conv_b, lru_wa, lru_ba, lru_wi,
                  lru_bi, lru_lambda, diff_lambda, attn_norm_g, w_out, post_g, post_b, router_g_w,
                  router_g_b, router_e_w, router_e_b, exp_w1, exp_w3, exp_w2, grid_w=GRID_W)
```

```python
import functools
import math

import jax
import jax.numpy as jnp
from jax import lax
from jax.experimental import pallas as pl
from jax.experimental.pallas import tpu as pltpu

F32 = jnp.float32
BF16 = jnp.bfloat16

LN_EPS = 1e-5
LRU_C = 8.0
ROPE_BASE = 10000.0
GRID_W = 64
HEAD_V_DIM = 128
HEAD_QK_DIM = HEAD_V_DIM // 2
CONV_W = 4
SUBLANES = 8
LANES = 128
NEG_BIG = -1e30
VMEM_LIMIT_BYTES = 56 * 1024 * 1024


def _cparams(*sem):
    return pltpu.CompilerParams(dimension_semantics=sem, vmem_limit_bytes=VMEM_LIMIT_BYTES)


def _layer_norm(x):
    mu = jnp.mean(x, axis=-1, keepdims=True)
    xc = x - mu
    var = jnp.mean(xc * xc, axis=-1, keepdims=True)
    return xc * lax.rsqrt(var + LN_EPS)


def _split_bf16(x):
    hi = x.astype(BF16)
    lo = (x - hi.astype(F32)).astype(BF16)
    return hi, lo


def _dot(a, b):
    return jnp.dot(a, b, preferred_element_type=F32)


def _dot_split(a_hi, a_lo, b_hi, b_lo):
    return _dot(a_hi, b_hi) + _dot(a_hi, b_lo) + _dot(a_lo, b_hi)


def _mods_kernel(c_ref, w_ref, b_ref, o_ref):
    c = c_ref[...]
    s = c * jax.nn.sigmoid(c)
    s_hi, s_lo = _split_bf16(s)
    w_hi, w_lo = _split_bf16(w_ref[...])
    o_ref[...] = _dot_split(s_hi, s_lo, w_hi, w_lo) + b_ref[...]


def _mods(cond, w, b):
    rows, d = cond.shape
    n = w.shape[1]
    tn = min(n, 1024)
    return pl.pallas_call(
        _mods_kernel,
        out_shape=jax.ShapeDtypeStruct((rows, n), F32),
        grid=(n // tn,),
        in_specs=[pl.BlockSpec((rows, d), lambda j: (0, 0)),
                  pl.BlockSpec((d, tn), lambda j: (0, j)),
                  pl.BlockSpec((1, tn), lambda j: (0, j))],
        out_specs=pl.BlockSpec((rows, tn), lambda j: (0, j)),
        compiler_params=_cparams("arbitrary"),
        name="mods",
    )(cond, w, b.reshape(1, n))


def _rope_tables(n_tokens, grid_w):
    n_freq = HEAD_QK_DIM // 4
    pos = jnp.arange(n_tokens, dtype=jnp.int32)
    row = (pos // grid_w).astype(F32)
    col = (pos % grid_w).astype(F32)
    inv = ROPE_BASE ** (-jnp.arange(n_freq, dtype=F32) / n_freq)
    ar = row[:, None] * inv
    ac = col[:, None] * inv
    cos64 = jnp.concatenate([jnp.cos(ar), jnp.cos(ar), jnp.cos(ac), jnp.cos(ac)], axis=1)
    sin64 = jnp.concatenate([-jnp.sin(ar), jnp.sin(ar), -jnp.sin(ac), jnp.sin(ac)], axis=1)
    reps = HEAD_V_DIM // HEAD_QK_DIM
    return jnp.tile(cos64, (1, reps)), jnp.tile(sin64, (1, reps))


def _inproj_kernel(x_ref, sh_ref, sc_ref, w_ref, cos_ref, sin_ref,
                   u_ref, g_ref, q_ref, k_ref, v_ref, *, lru_w, n_heads, rope):
    h = _layer_norm(x_ref[0]) * (1.0 + sc_ref[0]) + sh_ref[0]
    p = _dot(h.astype(BF16), w_ref[...])
    u_ref[0] = p[:, :lru_w]
    g_ref[0] = p[:, lru_w:2 * lru_w].astype(BF16)
    qk_w = n_heads * HEAD_V_DIM
    base_q = 2 * lru_w
    base_k = base_q + qk_w
    base_v = base_k + qk_w
    if rope:
        cos = cos_ref[...]
        sin = sin_ref[...]
        lane = lax.broadcasted_iota(jnp.int32, cos.shape, 1)
        first = (lane % 32) < 16
    q_scale = HEAD_QK_DIM ** -0.5 * math.log2(math.e)
    for hd in range(n_heads):
        lo, hi = hd * HEAD_V_DIM, (hd + 1) * HEAD_V_DIM
        for base, ref, scale in ((base_q, q_ref, q_scale), (base_k, k_ref, None)):
            t = p[:, base + lo:base + hi]
            if rope:
                partner = jnp.where(first, pltpu.roll(t, LANES - 16, 1), pltpu.roll(t, 16, 1))
                t = t * cos + partner * sin
            if scale is not None:
                t = t * scale
            ref[0, hd] = t.astype(BF16)
        v_ref[0, hd] = p[:, base_v + lo:base_v + hi].astype(BF16)


def _inproj(x, shift, scale, w_in, cos, sin, *, lru_w, n_heads, rope):
    bsz, seq, d = x.shape
    in_w = w_in.shape[1]
    tm = min(seq, 512)
    kern = functools.partial(_inproj_kernel, lru_w=lru_w, n_heads=n_heads, rope=rope)
    head_shape = jax.ShapeDtypeStruct((bsz, n_heads, seq, HEAD_V_DIM), BF16)
    head_spec = pl.BlockSpec((1, n_heads, tm, HEAD_V_DIM), lambda b, i: (b, 0, i, 0))
    vec_spec = pl.BlockSpec((1, 1, d), lambda b, i: (b, 0, 0))
    return pl.pallas_call(
        kern,
        out_shape=(jax.ShapeDtypeStruct((bsz, seq, lru_w), F32),
                   jax.ShapeDtypeStruct((bsz, seq, lru_w), BF16),
                   head_shape, head_shape, head_shape),
        grid=(bsz, seq // tm),
        in_specs=[pl.BlockSpec((1, tm, d), lambda b, i: (b, i, 0)),
                  vec_spec, vec_spec,
                  pl.BlockSpec((d, in_w), lambda b, i: (0, 0)),
                  pl.BlockSpec((tm, HEAD_V_DIM), lambda b, i: (i, 0)),
                  pl.BlockSpec((tm, HEAD_V_DIM), lambda b, i: (i, 0))],
        out_specs=(pl.BlockSpec((1, tm, lru_w), lambda b, i: (b, i, 0)),
                   pl.BlockSpec((1, tm, lru_w), lambda b, i: (b, i, 0)),
                   head_spec, head_spec, head_spec),
        compiler_params=_cparams("parallel", "arbitrary"),
        name="inproj_rope" if rope else "inproj_ctx",
    )(x, shift, scale, w_in, cos, sin)


def _scan_rows(a, b, reverse):
    row = lax.broadcasted_iota(jnp.int32, a.shape, 0)
    for s in (1, 2, 4):
        shift = SUBLANES - s if reverse else s
        a_sh = pltpu.roll(a, shift, 0)
        b_sh = pltpu.roll(b, shift, 0)
        valid = (row < SUBLANES - s) if reverse else (row >= s)
        b = jnp.where(valid, a * b_sh + b, b)
        a = jnp.where(valid, a * a_sh, a)
    return a, b


def _lru_kernel(*refs, reverse, combine, n_chunks, chunk):
    if combine:
        (u_ref, up_ref, un_ref, cw_ref, cb_ref, wg_ref, bg_ref, lam_ref, h0_ref, hb_ref, g_ref,
         out_ref, hn_ref, carry_ref, a_scr, b_scr) = refs
    else:
        (u_ref, up_ref, un_ref, cw_ref, cb_ref, wg_ref, bg_ref, lam_ref, h0_ref,
         out_ref, hn_ref, carry_ref, a_scr, b_scr) = refs
    c = pl.program_id(1)
    cc = (n_chunks - 1 - c) if reverse else c
    width = u_ref.shape[-1]

    @pl.when(c == 0)
    def _():
        carry_ref[...] = h0_ref[0]

    u = u_ref[0]
    prev = jnp.where(cc > 0, up_ref[0], 0.0)
    nxt = jnp.where(cc < n_chunks - 1, un_ref[0], 0.0)
    row = lax.broadcasted_iota(jnp.int32, u.shape, 0)
    u_m1 = jnp.where(row == 0, prev[7:8], pltpu.roll(u, 1, 0))
    u_m2 = jnp.where(row == 0, prev[6:7], jnp.where(row == 1, prev[7:8], pltpu.roll(u, 2, 0)))
    u_p1 = jnp.where(row == chunk - 1, nxt[0:1], pltpu.roll(u, chunk - 1, 0))
    cw = cw_ref[...]
    xc = cb_ref[...] + cw[0:1] * u_m2 + cw[1:2] * u_m1 + cw[2:3] * u + cw[3:4] * u_p1

    z = _dot(xc.astype(BF16), wg_ref[...]) + bg_ref[...]
    r = jax.nn.sigmoid(z[:, :width])
    i = jax.nn.sigmoid(z[:, width:])
    nlam = -lam_ref[...]
    softplus = jnp.maximum(nlam, 0.0) + jnp.log(1.0 + jnp.exp(-jnp.abs(nlam)))
    log_a = -LRU_C * r * softplus
    a = jnp.exp(log_a)
    a_scr[...] = a
    b_scr[...] = jnp.sqrt(1.0 - jnp.exp(2.0 * log_a)) * (i * xc)

    n_groups = chunk // SUBLANES

    def body(j, carry):
        jj = (n_groups - 1 - j) if reverse else j
        r0 = pl.multiple_of(jj * SUBLANES, SUBLANES)
        a_cum, h_loc = _scan_rows(a_scr[pl.ds(r0, SUBLANES), :], b_scr[pl.ds(r0, SUBLANES), :], reverse)
        h = a_cum * carry + h_loc
        b_scr[pl.ds(r0, SUBLANES), :] = h
        return h[0:1] if reverse else h[SUBLANES - 1:SUBLANES]

    carry = lax.fori_loop(0, n_groups, body, carry_ref[...])
    carry_ref[...] = carry
    hn_ref[0] = carry
    h_all = b_scr[...]
    if combine:
        out_ref[0] = (jax.nn.gelu(g_ref[0].astype(F32)) * (h_all + hb_ref[0])).astype(out_ref.dtype)
    else:
        out_ref[0] = h_all


def _lru(u, conv_w, conv_b, w_gate, b_gate, lam, h0, *, reverse, other=None, gate=None):
    bsz, seq, width = u.shape
    chunk = min(seq, 512)
    n_chunks = seq // chunk
    halo_per_chunk = chunk // SUBLANES
    n_halo = seq // SUBLANES
    combine = other is not None

    def pos(c):
        return (n_chunks - 1 - c) if reverse else c

    tile = lambda b, c: (b, pos(c), 0)
    prev_halo = lambda b, c: (b, jnp.maximum(pos(c) * halo_per_chunk - 1, 0), 0)
    next_halo = lambda b, c: (b, jnp.minimum((pos(c) + 1) * halo_per_chunk, n_halo - 1), 0)
    const2 = lambda b, c: (0, 0)
    state = lambda b, c: (b, 0, 0)
    in_specs = [pl.BlockSpec((1, chunk, width), tile),
                pl.BlockSpec((1, SUBLANES, width), prev_halo),
                pl.BlockSpec((1, SUBLANES, width), next_halo),
                pl.BlockSpec((CONV_W, width), const2),
                pl.BlockSpec((1, width), const2),
                pl.BlockSpec((width, 2 * width), const2),
                pl.BlockSpec((1, 2 * width), const2),
                pl.BlockSpec((1, width), const2),
                pl.BlockSpec((1, 1, width), state)]
    args = [u, u, u, conv_w, conv_b, w_gate, b_gate, lam, h0]
    if combine:
        in_specs += [pl.BlockSpec((1, chunk, width), tile), pl.BlockSpec((1, chunk, width), tile)]
        args += [other, gate]
    kern = functools.partial(_lru_kernel, reverse=reverse, combine=combine,
                             n_chunks=n_chunks, chunk=chunk)
    return pl.pallas_call(
        kern,
        out_shape=(jax.ShapeDtypeStruct((bsz, seq, width), BF16 if combine else F32),
                   jax.ShapeDtypeStruct((bsz, 1, width), F32)),
        grid=(bsz, n_chunks),
        in_specs=in_specs,
        out_specs=(pl.BlockSpec((1, chunk, width), tile), pl.BlockSpec((1, 1, width), state)),
        scratch_shapes=[pltpu.VMEM((1, width), F32),
                        pltpu.VMEM((chunk, width), F32),
                        pltpu.VMEM((chunk, width), F32)],
        compiler_params=_cparams("parallel", "arbitrary"),
        name=("lru_bwd" if reverse else "lru_fwd") + ("_mix" if combine else ""),
    )(*args)


def _expand_block_diag(w):
    n, k, _ = w.shape
    eye = jnp.eye(n, dtype=w.dtype)
    return (eye[:, None, :, None] * w[:, :, None, :]).reshape(n * k, n * k)


def _attn_kernel(dl_ref, q_ref, k_ref, v_ref, gn_ref, o_ref,
                 q2_scr, m_scr, acc_scr, s_scr, p_scr, a_scr, *, tk, rb, n_valid, lam_init):
    tq = q_ref.shape[2]
    dv = v_ref.shape[3]
    q = q_ref[0, 0]
    lane = lax.broadcasted_iota(jnp.int32, q.shape, 1)
    zero = jnp.zeros_like(q)
    q2_scr[:tq, :] = jnp.where(lane < HEAD_QK_DIM, q, zero)
    q2_scr[tq:, :] = jnp.where(lane >= HEAD_QK_DIM, q, zero)
    m_scr[...] = jnp.full(m_scr.shape, -jnp.inf, F32)
    acc_scr[...] = jnp.zeros(acc_scr.shape, F32)

    def kv_chunk(ref, j):
        start = j * tk if isinstance(j, int) else pl.multiple_of(j * tk, tk)
        return ref[0, 0, pl.ds(start, tk), :]

    def scores(j):
        return lax.dot_general(q2_scr[...], kv_chunk(k_ref, j), (((1,), (1,)), ((), ())),
                               preferred_element_type=F32)

    def softmax_rows(s_ref, p_ref, a_ref):
        for r in range(2 * tq // rb):
            rows = pl.ds(r * rb, rb)
            s = s_ref[rows, :]
            m_prev = m_scr[rows, :]
            m_new = jnp.maximum(m_prev, jnp.max(s, axis=-1, keepdims=True))
            p_ref[rows, :] = jnp.exp2(s - m_new).astype(BF16)
            a_ref[rows, :] = jnp.exp2(m_prev - m_new)
            m_scr[rows, :] = m_new

    def accumulate(buf, j):
        key = lax.broadcasted_iota(jnp.int32, (tk, dv), 0) + j * tk
        col = lax.broadcasted_iota(jnp.int32, (tk, dv), 1)
        valid = jnp.where((col == 0) & (key < n_valid), 1.0, 0.0).astype(BF16)
        v_ext = jnp.concatenate([kv_chunk(v_ref, j), valid], axis=1)
        acc_scr[...] = a_scr[buf] * acc_scr[...] + _dot(p_scr[buf], v_ext)

    n_chunks = k_ref.shape[2] // tk
    s_scr[0] = scores(0)
    p_scr[1] = jnp.zeros(p_scr.shape[1:], BF16)
    a_scr[1] = jnp.ones(a_scr.shape[1:], F32)

    def step(j, cur, with_scores):
        if with_scores:
            s_scr[1 - cur] = scores(j + 1)
        prev = max(j - 1, 0) if isinstance(j, int) else jnp.maximum(j - 1, 0)
        accumulate(1 - cur, prev)
        softmax_rows(s_scr.at[cur], p_scr.at[cur], a_scr.at[cur])

    def body(jj, carry):
        for par in range(2):
            step(2 * jj + par, par, True)
        return carry

    n_pairs = (n_chunks - 1) // 2
    lax.fori_loop(0, n_pairs, body, 0)
    for j in range(2 * n_pairs, n_chunks):
        step(j, j % 2, j + 1 < n_chunks)
    accumulate((n_chunks - 1) % 2, n_chunks - 1)

    lp = dl_ref[...]
    lam = (jnp.exp(jnp.sum(lp[0:1] * lp[1:2], axis=-1, keepdims=True))
           - jnp.exp(jnp.sum(lp[2:3] * lp[3:4], axis=-1, keepdims=True)) + lam_init)
    acc = acc_scr[...]
    o_all = acc[:, :dv] / acc[:, dv:dv + 1]
    o = o_all[:tq] - lam * o_all[tq:]
    y = o * lax.rsqrt(jnp.mean(o * o, axis=-1, keepdims=True) + LN_EPS) * (1.0 - lam_init)
    o_ref[0] = (y * gn_ref[0]).astype(o_ref.dtype)


def _attention(diff_lambda, q, k, v, k_c, v_c, norm_g, *, lam_init):
    bsz, n_heads, seq, dv = q.shape
    n_valid = seq + k_c.shape[2]
    tq = min(seq, 512)
    tk = min(seq // 2, 512)
    rb = min(2 * tq, 64)
    pad = -n_valid % tk
    k_all = jnp.concatenate([k, k_c, jnp.broadcast_to(k_c[:, :, :1], (bsz, n_heads, pad, dv))], axis=2)
    v_all = jnp.concatenate([v, v_c, jnp.zeros((bsz, n_heads, pad, dv), v.dtype)], axis=2)
    kern = functools.partial(_attn_kernel, tk=tk, rb=rb, n_valid=n_valid, lam_init=lam_init)
    kv_spec = pl.BlockSpec((1, 1, n_valid + pad, dv), lambda b, h, i: (b, h, 0, 0))
    return pl.pallas_call(
        kern,
        out_shape=jax.ShapeDtypeStruct((bsz, seq, n_heads * dv), BF16),
        grid=(bsz, n_heads, seq // tq),
        in_specs=[pl.BlockSpec(diff_lambda.shape, lambda b, h, i: (0, 0)),
                  pl.BlockSpec((1, 1, tq, dv), lambda b, h, i: (b, h, i, 0)),
                  kv_spec, kv_spec,
                  pl.BlockSpec((1, 1, dv), lambda b, h, i: (h, 0, 0))],
        out_specs=pl.BlockSpec((1, tq, dv), lambda b, h, i: (b, i, h)),
        scratch_shapes=[pltpu.VMEM((2 * tq, dv), BF16),
                        pltpu.VMEM((2 * tq, 1), F32),
                        pltpu.VMEM((2 * tq, 2 * dv), F32),
                        pltpu.VMEM((2, 2 * tq, tk), F32),
                        pltpu.VMEM((2, 2 * tq, tk), BF16),
                        pltpu.VMEM((2, 2 * tq, 1), F32)],
        compiler_params=_cparams("parallel", "parallel", "arbitrary"),
        name="diff_attn",
    )(diff_lambda, q, k_all, v_all, norm_g)


def _mid_kernel(x_ref, yl_ref, ya_ref, wo_ref, g1_ref, sh2_ref, sc2_ref, pg_ref, pb_ref,
                wrh_ref, wrl_ref, br_ref, xmid_ref, h2_ref, comb_ref,
                *, alpha, n_groups, per_group):
    lru_w = yl_ref.shape[-1]
    mix = _dot(yl_ref[0], wo_ref[:lru_w, :]) + _dot(ya_ref[0], wo_ref[lru_w:, :])
    x_mid = _layer_norm(alpha * x_ref[0] + g1_ref[0] * mix) * pg_ref[...] + pb_ref[...]
    xmid_ref[0] = x_mid
    h2 = _layer_norm(x_mid) * (1.0 + sc2_ref[0]) + sh2_ref[0]
    h2_ref[0] = h2.astype(BF16)

    h_hi, h_lo = _split_bf16(h2)
    logits = _dot_split(h_hi, h_lo, wrh_ref[...], wrl_ref[...]) + br_ref[...]
    lane = lax.broadcasted_iota(jnp.int32, logits.shape, 1)

    def first_argmax(vals, vmax):
        return jnp.min(jnp.where(vals == vmax, lane, LANES), axis=-1, keepdims=True)

    gl = jnp.where(lane < n_groups, logits, NEG_BIG)
    g_max = jnp.max(gl, axis=-1, keepdims=True)
    p_g = 1.0 / jnp.sum(jnp.exp(gl - g_max), axis=-1, keepdims=True)
    g_sel = first_argmax(gl, g_max)
    e_lo = n_groups + g_sel * per_group
    el = jnp.where((lane >= e_lo) & (lane < e_lo + per_group), logits, NEG_BIG)
    m1 = jnp.max(el, axis=-1, keepdims=True)
    i1 = first_argmax(el, m1)
    el2 = jnp.where(lane == i1, NEG_BIG, el)
    m2 = jnp.max(el2, axis=-1, keepdims=True)
    i2 = first_argmax(el2, m2)
    e2 = jnp.exp(m2 - m1)
    w1 = 1.0 / (1.0 + e2)
    w2 = e2 * w1
    comb_ref[0] = (jnp.where(lane == i1 - n_groups, p_g * w1, 0.0)
                   + jnp.where(lane == i2 - n_groups, p_g * w2, 0.0))


def _mid(x, y_lru, y_att, w_out, g1, sh2, sc2, post_g, post_b, wr_hi, wr_lo, b_r,
         *, alpha, n_groups, per_group):
    bsz, seq, d = x.shape
    lru_w = y_lru.shape[-1]
    att_w = y_att.shape[-1]
    tm = min(seq, 512)
    kern = functools.partial(_mid_kernel, alpha=alpha, n_groups=n_groups, per_group=per_group)
    tok = lambda w: pl.BlockSpec((1, tm, w), lambda b, i: (b, i, 0))
    vec = pl.BlockSpec((1, 1, d), lambda b, i: (b, 0, 0))
    const = lambda r, c: pl.BlockSpec((r, c), lambda b, i: (0, 0))
    return pl.pallas_call(
        kern,
        out_shape=(jax.ShapeDtypeStruct((bsz, seq, d), F32),
                   jax.ShapeDtypeStruct((bsz, seq, d), BF16),
                   jax.ShapeDtypeStruct((bsz, seq, LANES), F32)),
        grid=(bsz, seq // tm),
        in_specs=[tok(d), tok(lru_w), tok(att_w), const(lru_w + att_w, d), vec, vec, vec,
                  const(1, d), const(1, d), const(d, LANES), const(d, LANES), const(1, LANES)],
        out_specs=(tok(d), tok(d), tok(LANES)),
        compiler_params=_cparams("parallel", "arbitrary"),
        name="mid_router",
    )(x, y_lru, y_att, w_out, g1, sh2, sc2, post_g, post_b, wr_hi, wr_lo, b_r)


def _moe_kernel(h2_ref, comb_ref, w1_ref, w3_ref, w2_ref, xmid_ref, g2_ref, pg_ref, pb_ref,
                out_ref, acc_scr, *, alpha, n_experts):
    e = pl.program_id(2)

    @pl.when(e == 0)
    def _():
        acc_scr[...] = jnp.zeros(acc_scr.shape, F32)

    h = h2_ref[0]
    a = _dot(h, w1_ref[0])
    act = a * jax.nn.sigmoid(a) * _dot(h, w3_ref[0])
    y = _dot(act.astype(BF16), w2_ref[0])
    comb = comb_ref[0]
    lane = lax.broadcasted_iota(jnp.int32, comb.shape, 1)
    c_e = jnp.sum(jnp.where(lane == e, comb, 0.0), axis=-1, keepdims=True)
    acc_scr[...] += c_e * y

    @pl.when(e == n_experts - 1)
    def _():
        z = alpha * xmid_ref[0] + g2_ref[0] * acc_scr[...]
        out_ref[0] = _layer_norm(z) * pg_ref[...] + pb_ref[...]


def _moe(h2, comb, w1, w3, w2, x_mid, g2, post_g, post_b, *, alpha):
    bsz, seq, d = x_mid.shape
    n_experts, _, d_e = w1.shape
    tm = min(seq, 1024)
    kern = functools.partial(_moe_kernel, alpha=alpha, n_experts=n_experts)
    tok = lambda w: pl.BlockSpec((1, tm, w), lambda b, i, e: (b, i, 0))
    return pl.pallas_call(
        kern,
        out_shape=jax.ShapeDtypeStruct((bsz, seq, d), F32),
        grid=(bsz, seq // tm, n_experts),
        in_specs=[tok(d), tok(LANES),
                  pl.BlockSpec((1, d, d_e), lambda b, i, e: (e, 0, 0)),
                  pl.BlockSpec((1, d, d_e), lambda b, i, e: (e, 0, 0)),
                  pl.BlockSpec((1, d_e, d), lambda b, i, e: (e, 0, 0)),
                  tok(d),
                  pl.BlockSpec((1, 1, d), lambda b, i, e: (b, 0, 0)),
                  pl.BlockSpec((1, d), lambda b, i, e: (0, 0)),
                  pl.BlockSpec((1, d), lambda b, i, e: (0, 0))],
        out_specs=tok(d),
        scratch_shapes=[pltpu.VMEM((tm, d), F32)],
        compiler_params=_cparams("parallel", "parallel", "arbitrary"),
        name="moe_experts",
    )(h2, comb, w1, w3, w2, x_mid, g2, post_g, post_b)


def _block(x, c, ctx, c_ctx, w_mod, b_mod, w_in, conv_w, conv_b, lru_wa, lru_ba, lru_wi, lru_bi,
           lru_lambda, diff_lambda, attn_norm_g, w_out, post_g, post_b, router_g_w, router_g_b,
           router_e_w, router_e_b, exp_w1, exp_w3, exp_w2, *, grid_w):
    depth = w_mod.shape[0]
    assert depth == 1, "single-layer block only"
    bsz, seq, d = x.shape
    lru_w = conv_w.shape[-1]
    n_heads, dv = attn_norm_g.shape[1:]
    assert dv == HEAD_V_DIM and conv_w.shape[1] == CONV_W
    n_groups, _, per_group = router_e_w.shape[1:]
    assert n_groups + n_groups * per_group <= LANES
    alpha = (2.0 * depth) ** 0.25
    lam_init = 0.8 - 0.6 * math.exp(0.0)

    cond = jnp.concatenate([c, c_ctx[None], jnp.zeros((SUBLANES - 1 - bsz % SUBLANES, d), F32)], axis=0)
    mods = _mods(cond, w_mod[0], b_mod[0]).reshape(cond.shape[0], 6, 1, d)
    sh1, sc1, g1, sh2, sc2, g2 = (mods[:bsz, j] for j in range(6))
    sh1c, sc1c = (jnp.broadcast_to(mods[bsz:bsz + 1, j], (bsz, 1, d)) for j in range(2))

    w_in_b = w_in[0].astype(BF16)
    cos, sin = _rope_tables(seq, grid_w)
    u, gt, q, k, v = _inproj(x, sh1, sc1, w_in_b, cos, sin, lru_w=lru_w, n_heads=n_heads, rope=True)
    n_ctx = ctx.shape[1]
    u_c, _, _, k_c, v_c = _inproj(ctx, sh1c, sc1c, w_in_b, cos[:n_ctx], sin[:n_ctx],
                                  lru_w=lru_w, n_heads=n_heads, rope=False)

    zero_state = jnp.zeros((bsz, 1, lru_w), F32)
    cb = conv_b[0].reshape(1, lru_w)

    def gate_params(direction):
        w_gate = jnp.concatenate([_expand_block_diag(lru_wa[0, direction]),
                                  _expand_block_diag(lru_wi[0, direction])], axis=1).astype(BF16)
        b_gate = jnp.concatenate([lru_ba[0, direction], lru_bi[0, direction]]).reshape(1, 2 * lru_w)
        return w_gate, b_gate, lru_lambda[0, direction].reshape(1, lru_w)

    fwd_p, bwd_p = gate_params(0), gate_params(1)
    _, seed_f = _lru(u_c, conv_w[0], cb, *fwd_p, zero_state, reverse=False)
    _, seed_b = _lru(u_c, conv_w[0], cb, *bwd_p, zero_state, reverse=True)
    h_bwd, _ = _lru(u, conv_w[0], cb, *bwd_p, seed_b, reverse=True)
    y_lru, _ = _lru(u, conv_w[0], cb, *fwd_p, seed_f, reverse=False, other=h_bwd, gate=gt)

    y_att = _attention(diff_lambda[0], q, k, v, k_c, v_c, attn_norm_g[0].reshape(n_heads, 1, dv),
                       lam_init=lam_init)

    w_r = jnp.concatenate([router_g_w[0], jnp.moveaxis(router_e_w[0], 0, 1).reshape(d, -1)], axis=1)
    b_r = jnp.concatenate([router_g_b[0], router_e_b[0].reshape(-1)])
    pad = LANES - w_r.shape[1]
    w_r = jnp.pad(w_r, ((0, 0), (0, pad)))
    b_r = jnp.pad(b_r, (0, pad)).reshape(1, LANES)
    wr_hi, wr_lo = _split_bf16(w_r)

    x_mid, h2, comb = _mid(x, y_lru, y_att, w_out[0].astype(BF16), g1, sh2, sc2,
                           post_g[0, 0].reshape(1, d), post_b[0, 0].reshape(1, d),
                           wr_hi, wr_lo, b_r, alpha=alpha, n_groups=n_groups, per_group=per_group)
    return _moe(h2, comb, exp_w1[0].astype(BF16), exp_w3[0].astype(BF16), exp_w2[0].astype(BF16),
                x_mid, g2, post_g[0, 1].reshape(1, d), post_b[0, 1].reshape(1, d), alpha=alpha)


def kernel(x, c, ctx, c_ctx, w_mod, b_mod, w_in, conv_w, conv_b, lru_wa, lru_ba, lru_wi, lru_bi, lru_lambda, diff_lambda, attn_norm_g, w_out, post_g, post_b, router_g_w, router_g_b, router_e_w, router_e_b, exp_w1, exp_w3, exp_w2):
    return _block(x, c, ctx, c_ctx, w_mod, b_mod, w_in, conv_w, conv_b, lru_wa, lru_ba, lru_wi,
                  lru_bi, lru_lambda, diff_lambda, attn_norm_g, w_out, post_g, post_b, router_g_w,
                  router_g_b, router_e_w, router_e_b, exp_w1, exp_w3, exp_w2, grid_w=GRID_W)
```

```python
import functools
import math

import jax
import jax.numpy as jnp
from jax import lax
from jax.experimental import pallas as pl
from jax.experimental.pallas import tpu as pltpu

F32 = jnp.float32
BF16 = jnp.bfloat16

LN_EPS = 1e-5
LRU_C = 8.0
ROPE_BASE = 10000.0
GRID_W = 64
HEAD_V_DIM = 128
HEAD_QK_DIM = HEAD_V_DIM // 2
CONV_W = 4
SUBLANES = 8
LANES = 128
NEG_BIG = -1e30
VMEM_LIMIT_BYTES = 56 * 1024 * 1024


def _cparams(*sem):
    return pltpu.CompilerParams(dimension_semantics=sem, vmem_limit_bytes=VMEM_LIMIT_BYTES)


def _layer_norm(x):
    mu = jnp.mean(x, axis=-1, keepdims=True)
    xc = x - mu
    var = jnp.mean(xc * xc, axis=-1, keepdims=True)
    return xc * lax.rsqrt(var + LN_EPS)


def _split_bf16(x):
    hi = x.astype(BF16)
    lo = (x - hi.astype(F32)).astype(BF16)
    return hi, lo


def _dot(a, b):
    return jnp.dot(a, b, preferred_element_type=F32)


def _dot_split(a_hi, a_lo, b_hi, b_lo):
    return _dot(a_hi, b_hi) + _dot(a_hi, b_lo) + _dot(a_lo, b_hi)


def _mods_kernel(c_ref, w_ref, b_ref, o_ref):
    c = c_ref[...]
    s = c * jax.nn.sigmoid(c)
    s_hi, s_lo = _split_bf16(s)
    w_hi, w_lo = _split_bf16(w_ref[...])
    o_ref[...] = _dot_split(s_hi, s_lo, w_hi, w_lo) + b_ref[...]


def _mods(cond, w, b):
    rows, d = cond.shape
    n = w.shape[1]
    tn = min(n, 1024)
    return pl.pallas_call(
        _mods_kernel,
        out_shape=jax.ShapeDtypeStruct((rows, n), F32),
        grid=(n // tn,),
        in_specs=[pl.BlockSpec((rows, d), lambda j: (0, 0)),
                  pl.BlockSpec((d, tn), lambda j: (0, j)),
                  pl.BlockSpec((1, tn), lambda j: (0, j))],
        out_specs=pl.BlockSpec((rows, tn), lambda j: (0, j)),
        compiler_params=_cparams("arbitrary"),
        name="mods",
    )(cond, w, b.reshape(1, n))


def _rope_tables(n_tokens, grid_w):
    n_freq = HEAD_QK_DIM // 4
    pos = jnp.arange(n_tokens, dtype=jnp.int32)
    row = (pos // grid_w).astype(F32)
    col = (pos % grid_w).astype(F32)
    inv = ROPE_BASE ** (-jnp.arange(n_freq, dtype=F32) / n_freq)
    ar = row[:, None] * inv
    ac = col[:, None] * inv
    cos64 = jnp.concatenate([jnp.cos(ar), jnp.cos(ar), jnp.cos(ac), jnp.cos(ac)], axis=1)
    sin64 = jnp.concatenate([-jnp.sin(ar), jnp.sin(ar), -jnp.sin(ac), jnp.sin(ac)], axis=1)
    reps = HEAD_V_DIM // HEAD_QK_DIM
    return jnp.tile(cos64, (1, reps)), jnp.tile(sin64, (1, reps))


def _inproj_kernel(x_ref, sh_ref, sc_ref, w_ref, cos_ref, sin_ref,
                   u_ref, g_ref, q_ref, k_ref, v_ref, *, lru_w, n_heads, rope):
    h = _layer_norm(x_ref[0]) * (1.0 + sc_ref[0]) + sh_ref[0]
    p = _dot(h.astype(BF16), w_ref[...])
    u_ref[0] = p[:, :lru_w]
    g_ref[0] = p[:, lru_w:2 * lru_w].astype(BF16)
    qk_w = n_heads * HEAD_V_DIM
    base_q = 2 * lru_w
    base_k = base_q + qk_w
    base_v = base_k + qk_w
    if rope:
        cos = cos_ref[...]
        sin = sin_ref[...]
        lane = lax.broadcasted_iota(jnp.int32, cos.shape, 1)
        first = (lane % 32) < 16
    q_scale = HEAD_QK_DIM ** -0.5 * math.log2(math.e)
    for hd in range(n_heads):
        lo, hi = hd * HEAD_V_DIM, (hd + 1) * HEAD_V_DIM
        for base, ref, scale in ((base_q, q_ref, q_scale), (base_k, k_ref, None)):
            t = p[:, base + lo:base + hi]
            if rope:
                partner = jnp.where(first, pltpu.roll(t, LANES - 16, 1), pltpu.roll(t, 16, 1))
                t = t * cos + partner * sin
            if scale is not None:
                t = t * scale
            ref[0, hd] = t.astype(BF16)
        v_ref[0, hd] = p[:, base_v + lo:base_v + hi].astype(BF16)


def _inproj(x, shift, scale, w_in, cos, sin, *, lru_w, n_heads, rope):
    bsz, seq, d = x.shape
    in_w = w_in.shape[1]
    tm = min(seq, 512)
    kern = functools.partial(_inproj_kernel, lru_w=lru_w, n_heads=n_heads, rope=rope)
    head_shape = jax.ShapeDtypeStruct((bsz, n_heads, seq, HEAD_V_DIM), BF16)
    head_spec = pl.BlockSpec((1, n_heads, tm, HEAD_V_DIM), lambda b, i: (b, 0, i, 0))
    vec_spec = pl.BlockSpec((1, 1, d), lambda b, i: (b, 0, 0))
    return pl.pallas_call(
        kern,
        out_shape=(jax.ShapeDtypeStruct((bsz, seq, lru_w), F32),
                   jax.ShapeDtypeStruct((bsz, seq, lru_w), BF16),
                   head_shape, head_shape, head_shape),
        grid=(bsz, seq // tm),
        in_specs=[pl.BlockSpec((1, tm, d), lambda b, i: (b, i, 0)),
                  vec_spec, vec_spec,
                  pl.BlockSpec((d, in_w), lambda b, i: (0, 0)),
                  pl.BlockSpec((tm, HEAD_V_DIM), lambda b, i: (i, 0)),
                  pl.BlockSpec((tm, HEAD_V_DIM), lambda b, i: (i, 0))],
        out_specs=(pl.BlockSpec((1, tm, lru_w), lambda b, i: (b, i, 0)),
                   pl.BlockSpec((1, tm, lru_w), lambda b, i: (b, i, 0)),
                   head_spec, head_spec, head_spec),
        compiler_params=_cparams("parallel", "arbitrary"),
        name="inproj_rope" if rope else "inproj_ctx",
    )(x, shift, scale, w_in, cos, sin)


def _scan_rows(a, b, reverse):
    row = lax.broadcasted_iota(jnp.int32, a.shape, 0)
    for s in (1, 2, 4):
        shift = SUBLANES - s if reverse else s
        a_sh = pltpu.roll(a, shift, 0)
        b_sh = pltpu.roll(b, shift, 0)
        valid = (row < SUBLANES - s) if reverse else (row >= s)
        b = jnp.where(valid, a * b_sh + b, b)
        a = jnp.where(valid, a * a_sh, a)
    return a, b


def _lru_kernel(*refs, reverse, combine, n_chunks, chunk):
    if combine:
        (u_ref, up_ref, un_ref, cw_ref, cb_ref, wg_ref, bg_ref, lam_ref, h0_ref, hb_ref, g_ref,
         out_ref, hn_ref, carry_ref, a_scr, b_scr) = refs
    else:
        (u_ref, up_ref, un_ref, cw_ref, cb_ref, wg_ref, bg_ref, lam_ref, h0_ref,
         out_ref, hn_ref, carry_ref, a_scr, b_scr) = refs
    c = pl.program_id(1)
    cc = (n_chunks - 1 - c) if reverse else c
    width = u_ref.shape[-1]

    @pl.when(c == 0)
    def _():
        carry_ref[...] = h0_ref[0]

    u = u_ref[0]
    prev = jnp.where(cc > 0, up_ref[0], 0.0)
    nxt = jnp.where(cc < n_chunks - 1, un_ref[0], 0.0)
    row = lax.broadcasted_iota(jnp.int32, u.shape, 0)
    u_m1 = jnp.where(row == 0, prev[7:8], pltpu.roll(u, 1, 0))
    u_m2 = jnp.where(row == 0, prev[6:7], jnp.where(row == 1, prev[7:8], pltpu.roll(u, 2, 0)))
    u_p1 = jnp.where(row == chunk - 1, nxt[0:1], pltpu.roll(u, chunk - 1, 0))
    cw = cw_ref[...]
    xc = cb_ref[...] + cw[0:1] * u_m2 + cw[1:2] * u_m1 + cw[2:3] * u + cw[3:4] * u_p1

    z = _dot(xc.astype(BF16), wg_ref[...]) + bg_ref[...]
    r = jax.nn.sigmoid(z[:, :width])
    i = jax.nn.sigmoid(z[:, width:])
    nlam = -lam_ref[...]
    softplus = jnp.maximum(nlam, 0.0) + jnp.log(1.0 + jnp.exp(-jnp.abs(nlam)))
    log_a = -LRU_C * r * softplus
    a = jnp.exp(log_a)
    a_scr[...] = a
    b_scr[...] = jnp.sqrt(1.0 - jnp.exp(2.0 * log_a)) * (i * xc)

    n_groups = chunk // SUBLANES

    def body(j, carry):
        jj = (n_groups - 1 - j) if reverse else j
        r0 = pl.multiple_of(jj * SUBLANES, SUBLANES)
        a_cum, h_loc = _scan_rows(a_scr[pl.ds(r0, SUBLANES), :], b_scr[pl.ds(r0, SUBLANES), :], reverse)
        h = a_cum * carry + h_loc
        b_scr[pl.ds(r0, SUBLANES), :] = h
        return h[0:1] if reverse else h[SUBLANES - 1:SUBLANES]

    carry = lax.fori_loop(0, n_groups, body, carry_ref[...])
    carry_ref[...] = carry
    hn_ref[0] = carry
    h_all = b_scr[...]
    if combine:
        out_ref[0] = (jax.nn.gelu(g_ref[0].astype(F32)) * (h_all + hb_ref[0])).astype(out_ref.dtype)
    else:
        out_ref[0] = h_all


def _lru(u, conv_w, conv_b, w_gate, b_gate, lam, h0, *, reverse, other=None, gate=None):
    bsz, seq, width = u.shape
    chunk = min(seq, 512)
    n_chunks = seq // chunk
    halo_per_chunk = chunk // SUBLANES
    n_halo = seq // SUBLANES
    combine = other is not None

    def pos(c):
        return (n_chunks - 1 - c) if reverse else c

    tile = lambda b, c: (b, pos(c), 0)
    prev_halo = lambda b, c: (b, jnp.maximum(pos(c) * halo_per_chunk - 1, 0), 0)
    next_halo = lambda b, c: (b, jnp.minimum((pos(c) + 1) * halo_per_chunk, n_halo - 1), 0)
    const2 = lambda b, c: (0, 0)
    state = lambda b, c: (b, 0, 0)
    in_specs = [pl.BlockSpec((1, chunk, width), tile),
                pl.BlockSpec((1, SUBLANES, width), prev_halo),
                pl.BlockSpec((1, SUBLANES, width), next_halo),
                pl.BlockSpec((CONV_W, width), const2),
                pl.BlockSpec((1, width), const2),
                pl.BlockSpec((width, 2 * width), const2),
                pl.BlockSpec((1, 2 * width), const2),
                pl.BlockSpec((1, width), const2),
                pl.BlockSpec((1, 1, width), state)]
    args = [u, u, u, conv_w, conv_b, w_gate, b_gate, lam, h0]
    if combine:
        in_specs += [pl.BlockSpec((1, chunk, width), tile), pl.BlockSpec((1, chunk, width), tile)]
        args += [other, gate]
    kern = functools.partial(_lru_kernel, reverse=reverse, combine=combine,
                             n_chunks=n_chunks, chunk=chunk)
    return pl.pallas_call(
        kern,
        out_shape=(jax.ShapeDtypeStruct((bsz, seq, width), BF16 if combine else F32),
                   jax.ShapeDtypeStruct((bsz, 1, width), F32)),
        grid=(bsz, n_chunks),
        in_specs=in_specs,
        out_specs=(pl.BlockSpec((1, chunk, width), tile), pl.BlockSpec((1, 1, width), state)),
        scratch_shapes=[pltpu.VMEM((1, width), F32),
                        pltpu.VMEM((chunk, width), F32),
                        pltpu.VMEM((chunk, width), F32)],
        compiler_params=_cparams("parallel", "arbitrary"),
        name=("lru_bwd" if reverse else "lru_fwd") + ("_mix" if combine else ""),
    )(*args)


def _expand_block_diag(w):
    n, k, _ = w.shape
    eye = jnp.eye(n, dtype=w.dtype)
    return (eye[:, None, :, None] * w[:, :, None, :]).reshape(n * k, n * k)


def _attn_kernel(dl_ref, q_ref, k_ref, v_ref, gn_ref, o_ref,
                 q2_scr, m_scr, acc_scr, s_scr, p_scr, a_scr, *, tk, rb, n_valid, lam_init):
    tq = q_ref.shape[2]
    dv = v_ref.shape[3]
    q = q_ref[0, 0]
    lane = lax.broadcasted_iota(jnp.int32, q.shape, 1)
    zero = jnp.zeros_like(q)
    q2_scr[:tq, :] = jnp.where(lane < HEAD_QK_DIM, q, zero)
    q2_scr[tq:, :] = jnp.where(lane >= HEAD_QK_DIM, q, zero)
    m_scr[...] = jnp.full(m_scr.shape, -jnp.inf, F32)
    acc_scr[...] = jnp.zeros(acc_scr.shape, F32)

    def kv_chunk(ref, j):
        start = j * tk if isinstance(j, int) else pl.multiple_of(j * tk, tk)
        return ref[0, 0, pl.ds(start, tk), :]

    def scores(j):
        return lax.dot_general(q2_scr[...], kv_chunk(k_ref, j), (((1,), (1,)), ((), ())),
                               preferred_element_type=F32)

    def softmax_rows(s_ref, p_ref, a_ref):
        for r in range(2 * tq // rb):
            rows = pl.ds(r * rb, rb)
            s = s_ref[rows, :]
            m_prev = m_scr[rows, :]
            m_new = jnp.maximum(m_prev, jnp.max(s, axis=-1, keepdims=True))
            p_ref[rows, :] = jnp.exp2(s - m_new).astype(BF16)
            a_ref[rows, :] = jnp.exp2(m_prev - m_new)
            m_scr[rows, :] = m_new

    def accumulate(buf, j):
        key = lax.broadcasted_iota(jnp.int32, (tk, dv), 0) + j * tk
        col = lax.broadcasted_iota(jnp.int32, (tk, dv), 1)
        valid = jnp.where((col == 0) & (key < n_valid), 1.0, 0.0).astype(BF16)
        v_ext = jnp.concatenate([kv_chunk(v_ref, j), valid], axis=1)
        acc_scr[...] = a_scr[buf] * acc_scr[...] + _dot(p_scr[buf], v_ext)

    n_chunks = k_ref.shape[2] // tk
    s_scr[0] = scores(0)
    p_scr[1] = jnp.zeros(p_scr.shape[1:], BF16)
    a_scr[1] = jnp.ones(a_scr.shape[1:], F32)

    def step(j, cur, with_scores):
        if with_scores:
            s_scr[1 - cur] = scores(j + 1)
        prev = max(j - 1, 0) if isinstance(j, int) else jnp.maximum(j - 1, 0)
        accumulate(1 - cur, prev)
        softmax_rows(s_scr.at[cur], p_scr.at[cur], a_scr.at[cur])

    def body(jj, carry):
        for par in range(2):
            step(2 * jj + par, par, True)
        return carry

    n_pairs = (n_chunks - 1) // 2
    lax.fori_loop(0, n_pairs, body, 0)
    for j in range(2 * n_pairs, n_chunks):
        step(j, j % 2, j + 1 < n_chunks)
    accumulate((n_chunks - 1) % 2, n_chunks - 1)

    lp = dl_ref[...]
    lam = (jnp.exp(jnp.sum(lp[0:1] * lp[1:2], axis=-1, keepdims=True))
           - jnp.exp(jnp.sum(lp[2:3] * lp[3:4], axis=-1, keepdims=True)) + lam_init)
    acc = acc_scr[...]
    o_all = acc[:, :dv] / acc[:, dv:dv + 1]
    o = o_all[:tq] - lam * o_all[tq:]
    y = o * lax.rsqrt(jnp.mean(o * o, axis=-1, keepdims=True) + LN_EPS) * (1.0 - lam_init)
    o_ref[0] = (y * gn_ref[0]).astype(o_ref.dtype)


def _attention(diff_lambda, q, k, v, k_c, v_c, norm_g, *, lam_init):
    bsz, n_heads, seq, dv = q.shape
    n_valid = seq + k_c.shape[2]
    tq = min(seq, 512)
    tk = min(seq // 2, 512)
    rb = min(2 * tq, 64)
    pad = -n_valid % tk
    k_all = jnp.concatenate([k, k_c, jnp.broadcast_to(k_c[:, :, :1], (bsz, n_heads, pad, dv))], axis=2)
    v_all = jnp.concatenate([v, v_c, jnp.zeros((bsz, n_heads, pad, dv), v.dtype)], axis=2)
    kern = functools.partial(_attn_kernel, tk=tk, rb=rb, n_valid=n_valid, lam_init=lam_init)
    kv_spec = pl.BlockSpec((1, 1, n_valid + pad, dv), lambda b, h, i: (b, h, 0, 0))
    return pl.pallas_call(
        kern,
        out_shape=jax.ShapeDtypeStruct((bsz, seq, n_heads * dv), BF16),
        grid=(bsz, n_heads, seq // tq),
        in_specs=[pl.BlockSpec(diff_lambda.shape, lambda b, h, i: (0, 0)),
                  pl.BlockSpec((1, 1, tq, dv), lambda b, h, i: (b, h, i, 0)),
                  kv_spec, kv_spec,
                  pl.BlockSpec((1, 1, dv), lambda b, h, i: (h, 0, 0))],
        out_specs=pl.BlockSpec((1, tq, dv), lambda b, h, i: (b, i, h)),
        scratch_shapes=[pltpu.VMEM((2 * tq, dv), BF16),
                        pltpu.VMEM((2 * tq, 1), F32),
                        pltpu.VMEM((2 * tq, 2 * dv), F32),
                        pltpu.VMEM((2, 2 * tq, tk), F32),
                        pltpu.VMEM((2, 2 * tq, tk), BF16),
                        pltpu.VMEM((2, 2 * tq, 1), F32)],
        compiler_params=_cparams("parallel", "parallel", "arbitrary"),
        name="diff_attn",
    )(diff_lambda, q, k_all, v_all, norm_g)


def _mid_kernel(x_ref, yl_ref, ya_ref, wo_ref, g1_ref, sh2_ref, sc2_ref, pg_ref, pb_ref,
                wrh_ref, wrl_ref, br_ref, xmid_ref, h2_ref, route_ref, counts_ref, cnt_scr,
                *, alpha, n_groups, per_group):
    @pl.when((pl.program_id(0) == 0) & (pl.program_id(1) == 0))
    def _():
        cnt_scr[...] = jnp.zeros(cnt_scr.shape, F32)

    lru_w = yl_ref.shape[-1]
    mix = _dot(yl_ref[0], wo_ref[:lru_w, :]) + _dot(ya_ref[0], wo_ref[lru_w:, :])
    x_mid = _layer_norm(alpha * x_ref[0] + g1_ref[0] * mix) * pg_ref[...] + pb_ref[...]
    xmid_ref[0] = x_mid
    h2 = _layer_norm(x_mid) * (1.0 + sc2_ref[0]) + sh2_ref[0]
    h2_ref[0] = h2

    h_hi, h_lo = _split_bf16(h2)
    logits = _dot_split(h_hi, h_lo, wrh_ref[...], wrl_ref[...]) + br_ref[...]
    lane = lax.broadcasted_iota(jnp.int32, logits.shape, 1)

    def first_argmax(vals, vmax):
        return jnp.min(jnp.where(vals == vmax, lane, LANES), axis=-1, keepdims=True)

    gl = jnp.where(lane < n_groups, logits, NEG_BIG)
    g_max = jnp.max(gl, axis=-1, keepdims=True)
    p_g = 1.0 / jnp.sum(jnp.exp(gl - g_max), axis=-1, keepdims=True)
    g_sel = first_argmax(gl, g_max)
    e_lo = n_groups + g_sel * per_group
    el = jnp.where((lane >= e_lo) & (lane < e_lo + per_group), logits, NEG_BIG)
    m1 = jnp.max(el, axis=-1, keepdims=True)
    i1 = first_argmax(el, m1)
    el2 = jnp.where(lane == i1, NEG_BIG, el)
    m2 = jnp.max(el2, axis=-1, keepdims=True)
    i2 = first_argmax(el2, m2)
    e2 = jnp.exp(m2 - m1)
    w1 = 1.0 / (1.0 + e2)
    w2 = e2 * w1

    ex1 = i1 - n_groups
    ex2 = i2 - n_groups
    oh1 = jnp.where(lane == ex1, 1.0, 0.0)
    oh2 = jnp.where(lane == ex2, 1.0, 0.0)
    both = oh1 + oh2
    tm = both.shape[0]
    tri = (lax.broadcasted_iota(jnp.int32, (tm, tm), 0)
           > lax.broadcasted_iota(jnp.int32, (tm, tm), 1)).astype(BF16)
    before = _dot(tri, both.astype(BF16)) + cnt_scr[...]
    rank1 = jnp.sum(oh1 * before, axis=-1, keepdims=True)
    rank2 = jnp.sum(oh2 * before, axis=-1, keepdims=True)
    cnt_scr[...] += jnp.sum(both, axis=0, keepdims=True)
    counts_ref[...] = cnt_scr[...]

    fields = (ex1.astype(F32), ex2.astype(F32), p_g * w1, p_g * w2, rank1, rank2)
    slab = jnp.zeros(logits.shape, F32)
    for idx, val in enumerate(fields):
        slab = jnp.where(lane == idx, val, slab)
    route_ref[0] = slab


def _mid(x, y_lru, y_att, w_out, g1, sh2, sc2, post_g, post_b, wr_hi, wr_lo, b_r,
         *, alpha, n_groups, per_group):
    bsz, seq, d = x.shape
    lru_w = y_lru.shape[-1]
    att_w = y_att.shape[-1]
    tm = min(seq, 512)
    kern = functools.partial(_mid_kernel, alpha=alpha, n_groups=n_groups, per_group=per_group)
    tok = lambda w: pl.BlockSpec((1, tm, w), lambda b, i: (b, i, 0))
    vec = pl.BlockSpec((1, 1, d), lambda b, i: (b, 0, 0))
    const = lambda r, c: pl.BlockSpec((r, c), lambda b, i: (0, 0))
    return pl.pallas_call(
        kern,
        out_shape=(jax.ShapeDtypeStruct((bsz, seq, d), F32),
                   jax.ShapeDtypeStruct((bsz, seq, d), F32),
                   jax.ShapeDtypeStruct((bsz, seq, LANES), F32),
                   jax.ShapeDtypeStruct((1, LANES), F32)),
        grid=(bsz, seq // tm),
        in_specs=[tok(d), tok(lru_w), tok(att_w), const(lru_w + att_w, d), vec, vec, vec,
                  const(1, d), const(1, d), const(d, LANES), const(d, LANES), const(1, LANES)],
        out_specs=(tok(d), tok(d), tok(LANES), const(1, LANES)),
        scratch_shapes=[pltpu.VMEM((1, LANES), F32)],
        compiler_params=_cparams("arbitrary", "arbitrary"),
        name="mid_router",
    )(x, y_lru, y_att, w_out, g1, sh2, sc2, post_g, post_b, wr_hi, wr_lo, b_r)


def _row_copy(src, src_row, dst, dst_row, sem):
    return pltpu.make_async_copy(src.at[pl.ds(src_row, 1)], dst.at[pl.ds(dst_row, 1)], sem)


def _drain_rows(src, dst, sem, n):
    pltpu.make_async_copy(src.at[pl.ds(0, n)], dst.at[pl.ds(0, n)], sem).wait()


def _dispatch_kernel(slot_ref, h_hbm, init_hbm, out_hbm, sem, *, tile):
    del init_hbm
    base = pl.program_id(0) * tile

    def issue(t, carry):
        for k in range(2):
            _row_copy(h_hbm, base + t, out_hbm, slot_ref[0, 0, 2 * t + k], sem).start()
        return carry

    lax.fori_loop(0, tile, issue, 0)
    _drain_rows(h_hbm, out_hbm, sem, 2 * tile)


def _dispatch(h2, slots, n_rows, *, tile):
    n, d = h2.shape
    n_steps = n // tile
    return pl.pallas_call(
        functools.partial(_dispatch_kernel, tile=tile),
        out_shape=jax.ShapeDtypeStruct((n_rows, d), F32),
        grid=(n_steps,),
        in_specs=[pl.BlockSpec((1, 1, 2 * tile), lambda i: (i, 0, 0), memory_space=pltpu.SMEM),
                  pl.BlockSpec(memory_space=pl.ANY),
                  pl.BlockSpec(memory_space=pl.ANY)],
        out_specs=pl.BlockSpec(memory_space=pl.ANY),
        scratch_shapes=[pltpu.SemaphoreType.DMA(())],
        input_output_aliases={2: 0},
        compiler_params=_cparams("arbitrary"),
        name="moe_dispatch",
    )(slots.reshape(n_steps, 1, 2 * tile), h2, jnp.zeros((n_rows, d), F32))


def _experts_kernel(te_ref, nu_ref, h_ref, w1_ref, w3_ref, w2_ref, y_ref):
    del te_ref
    used = pl.program_id(0) < nu_ref[0]

    @pl.when(used)
    def _():
        h = h_ref[...].astype(BF16)
        a = _dot(h, w1_ref[0])
        act = a * jax.nn.sigmoid(a) * _dot(h, w3_ref[0])
        y_ref[...] = _dot(act.astype(BF16), w2_ref[0])

    @pl.when(jnp.logical_not(used))
    def _():
        y_ref[...] = jnp.zeros(y_ref.shape, F32)


def _experts(hs, tile_expert, n_used, w1, w3, w2, *, tile):
    n_rows, d = hs.shape
    d_e = w1.shape[2]
    used = lambda i, te, nu: jnp.minimum(i, nu[0] - 1)
    row_spec = pl.BlockSpec((tile, d), lambda i, te, nu: (used(i, te, nu), 0))
    w_spec = lambda r, c: pl.BlockSpec((1, r, c), lambda i, te, nu: (te[used(i, te, nu)], 0, 0))
    return pl.pallas_call(
        _experts_kernel,
        out_shape=jax.ShapeDtypeStruct((n_rows, d), F32),
        grid_spec=pltpu.PrefetchScalarGridSpec(
            num_scalar_prefetch=2,
            grid=(n_rows // tile,),
            in_specs=[row_spec, w_spec(d, d_e), w_spec(d, d_e), w_spec(d_e, d)],
            out_specs=pl.BlockSpec((tile, d), lambda i, te, nu: (i, 0))),
        compiler_params=_cparams("arbitrary"),
        name="moe_experts",
    )(tile_expert, n_used, hs, w1, w3, w2)


def _combine_kernel(slot_ref, route_ref, y_hbm, xmid_ref, g2_ref, pg_ref, pb_ref, out_ref,
                    ya_scr, yb_scr, sem, *, alpha, tile):
    def issue(t, carry):
        _row_copy(y_hbm, slot_ref[0, 0, 2 * t], ya_scr, t, sem).start()
        _row_copy(y_hbm, slot_ref[0, 0, 2 * t + 1], yb_scr, t, sem).start()
        return carry

    lax.fori_loop(0, tile, issue, 0)
    _drain_rows(y_hbm, ya_scr, sem, tile)
    _drain_rows(y_hbm, yb_scr, sem, tile)
    route = route_ref[...]
    y = route[:, 2:3] * ya_scr[...] + route[:, 3:4] * yb_scr[...]
    z = alpha * xmid_ref[...] + g2_ref[0] * y
    out_ref[...] = _layer_norm(z) * pg_ref[...] + pb_ref[...]


def _combine(ys, slots, route, x_mid, g2, post_g, post_b, *, alpha, seq, tile):
    n, d = x_mid.shape
    n_steps = n // tile
    assert seq % tile == 0
    tok = lambda w: pl.BlockSpec((tile, w), lambda i: (i, 0))
    const = pl.BlockSpec((1, d), lambda i: (0, 0))
    return pl.pallas_call(
        functools.partial(_combine_kernel, alpha=alpha, tile=tile),
        out_shape=jax.ShapeDtypeStruct((n, d), F32),
        grid=(n_steps,),
        in_specs=[pl.BlockSpec((1, 1, 2 * tile), lambda i: (i, 0, 0), memory_space=pltpu.SMEM),
                  tok(LANES),
                  pl.BlockSpec(memory_space=pl.ANY),
                  tok(d),
                  pl.BlockSpec((1, 1, d), lambda i: (i * tile // seq, 0, 0)),
                  const, const],
        out_specs=tok(d),
        scratch_shapes=[pltpu.VMEM((tile, d), F32), pltpu.VMEM((tile, d), F32),
                        pltpu.SemaphoreType.DMA(())],
        compiler_params=_cparams("arbitrary"),
        name="moe_combine",
    )(slots.reshape(n_steps, 1, 2 * tile), route, ys, x_mid, g2, post_g, post_b)


def _moe(h2, route, counts, w1, w3, w2, x_mid, g2, post_g, post_b, *, alpha):
    bsz, seq, d = x_mid.shape
    n_experts = w1.shape[0]
    n = bsz * seq
    tile_e = 256
    tile_t = min(seq, 256)
    n_tiles = 2 * n // tile_e + n_experts
    route = route.reshape(n, LANES)
    expert = route[:, 0:2].astype(jnp.int32)
    rank = route[:, 4:6].astype(jnp.int32)
    cnt = counts[0, :n_experts].astype(jnp.int32)
    tiles = (cnt + tile_e - 1) // tile_e
    tile_end = jnp.cumsum(tiles)
    slots = ((tile_end - tiles) * tile_e)[expert] + rank
    tile_ids = jnp.arange(n_tiles, dtype=jnp.int32)
    tile_expert = jnp.minimum(jnp.sum((tile_end[None, :] <= tile_ids[:, None]).astype(jnp.int32), axis=1),
                              n_experts - 1)
    n_used = tile_end[-1:].astype(jnp.int32)
    hs = _dispatch(h2.reshape(n, d), slots, n_tiles * tile_e, tile=tile_t)
    ys = _experts(hs, tile_expert, n_used, w1, w3, w2, tile=tile_e)
    out = _combine(ys, slots, route, x_mid.reshape(n, d), g2, post_g, post_b,
                   alpha=alpha, seq=seq, tile=tile_t)
    return out.reshape(bsz, seq, d)


def _block(x, c, ctx, c_ctx, w_mod, b_mod, w_in, conv_w, conv_b, lru_wa, lru_ba, lru_wi, lru_bi,
           lru_lambda, diff_lambda, attn_norm_g, w_out, post_g, post_b, router_g_w, router_g_b,
           router_e_w, router_e_b, exp_w1, exp_w3, exp_w2, *, grid_w):
    depth = w_mod.shape[0]
    assert depth == 1, "single-layer block only"
    bsz, seq, d = x.shape
    lru_w = conv_w.shape[-1]
    n_heads, dv = attn_norm_g.shape[1:]
    assert dv == HEAD_V_DIM and conv_w.shape[1] == CONV_W
    n_groups, _, per_group = router_e_w.shape[1:]
    assert n_groups + n_groups * per_group <= LANES
    alpha = (2.0 * depth) ** 0.25
    lam_init = 0.8 - 0.6 * math.exp(0.0)

    cond = jnp.concatenate([c, c_ctx[None], jnp.zeros((SUBLANES - 1 - bsz % SUBLANES, d), F32)], axis=0)
    mods = _mods(cond, w_mod[0], b_mod[0]).reshape(cond.shape[0], 6, 1, d)
    sh1, sc1, g1, sh2, sc2, g2 = (mods[:bsz, j] for j in range(6))
    sh1c, sc1c = (jnp.broadcast_to(mods[bsz:bsz + 1, j], (bsz, 1, d)) for j in range(2))

    w_in_b = w_in[0].astype(BF16)
    cos, sin = _rope_tables(seq, grid_w)
    u, gt, q, k, v = _inproj(x, sh1, sc1, w_in_b, cos, sin, lru_w=lru_w, n_heads=n_heads, rope=True)
    n_ctx = ctx.shape[1]
    u_c, _, _, k_c, v_c = _inproj(ctx, sh1c, sc1c, w_in_b, cos[:n_ctx], sin[:n_ctx],
                                  lru_w=lru_w, n_heads=n_heads, rope=False)

    zero_state = jnp.zeros((bsz, 1, lru_w), F32)
    cb = conv_b[0].reshape(1, lru_w)

    def gate_params(direction):
        w_gate = jnp.concatenate([_expand_block_diag(lru_wa[0, direction]),
                                  _expand_block_diag(lru_wi[0, direction])], axis=1).astype(BF16)
        b_gate = jnp.concatenate([lru_ba[0, direction], lru_bi[0, direction]]).reshape(1, 2 * lru_w)
        return w_gate, b_gate, lru_lambda[0, direction].reshape(1, lru_w)

    fwd_p, bwd_p = gate_params(0), gate_params(1)
    _, seed_f = _lru(u_c, conv_w[0], cb, *fwd_p, zero_state, reverse=False)
    _, seed_b = _lru(u_c, conv_w[0], cb, *bwd_p, zero_state, reverse=True)
    h_bwd, _ = _lru(u, conv_w[0], cb, *bwd_p, seed_b, reverse=True)
    y_lru, _ = _lru(u, conv_w[0], cb, *fwd_p, seed_f, reverse=False, other=h_bwd, gate=gt)

    y_att = _attention(diff_lambda[0], q, k, v, k_c, v_c, attn_norm_g[0].reshape(n_heads, 1, dv),
                       lam_init=lam_init)

    w_r = jnp.concatenate([router_g_w[0], jnp.moveaxis(router_e_w[0], 0, 1).reshape(d, -1)], axis=1)
    b_r = jnp.concatenate([router_g_b[0], router_e_b[0].reshape(-1)])
    pad = LANES - w_r.shape[1]
    w_r = jnp.pad(w_r, ((0, 0), (0, pad)))
    b_r = jnp.pad(b_r, (0, pad)).reshape(1, LANES)
    wr_hi, wr_lo = _split_bf16(w_r)

    x_mid, h2, route, counts = _mid(x, y_lru, y_att, w_out[0].astype(BF16), g1, sh2, sc2,
                                    post_g[0, 0].reshape(1, d), post_b[0, 0].reshape(1, d),
                                    wr_hi, wr_lo, b_r, alpha=alpha, n_groups=n_groups,
                                    per_group=per_group)
    return _moe(h2, route, counts, exp_w1[0].astype(BF16), exp_w3[0].astype(BF16),
                exp_w2[0].astype(BF16), x_mid, g2, post_g[0, 1].reshape(1, d),
                post_b[0, 1].reshape(1, d), alpha=alpha)


def kernel(x, c, ctx, c_ctx, w_mod, b_mod, w_in, conv_w, conv_b, lru_wa, lru_ba, lru_wi, lru_bi, lru_lambda, diff_lambda, attn_norm_g, w_out, post_g, post_b, router_g_w, router_g_b, router_e_w, router_e_b, exp_w1, exp_w3, exp_w2):
    return _block(x, c, ctx, c_ctx, w_mod, b_mod, w_in, conv_w, conv_b, lru_wa, lru_ba, lru_wi,
                  lru_bi, lru_lambda, diff_lambda, attn_norm_g, w_out, post_g, post_b, router_g_w,
                  router_g_b, router_e_w, router_e_b, exp_w1, exp_w3, exp_w2, grid_w=GRID_W)
```

```python
import functools
import math

import jax
import jax.numpy as jnp
from jax import lax
from jax.experimental import pallas as pl
from jax.experimental.pallas import tpu as pltpu

F32 = jnp.float32
BF16 = jnp.bfloat16

LN_EPS = 1e-5
LRU_C = 8.0
ROPE_BASE = 10000.0
GRID_W = 64
HEAD_V_DIM = 128
HEAD_QK_DIM = HEAD_V_DIM // 2
CONV_W = 4
SUBLANES = 8
LANES = 128
NEG_BIG = -1e30
VMEM_LIMIT_BYTES = 56 * 1024 * 1024


def _cparams(*sem):
    return pltpu.CompilerParams(dimension_semantics=sem, vmem_limit_bytes=VMEM_LIMIT_BYTES)


def _layer_norm(x):
    mu = jnp.mean(x, axis=-1, keepdims=True)
    xc = x - mu
    var = jnp.mean(xc * xc, axis=-1, keepdims=True)
    return xc * lax.rsqrt(var + LN_EPS)


def _split_bf16(x):
    hi = x.astype(BF16)
    lo = (x - hi.astype(F32)).astype(BF16)
    return hi, lo


def _dot(a, b):
    return jnp.dot(a, b, preferred_element_type=F32)


def _dot_split(a_hi, a_lo, b_hi, b_lo):
    return _dot(a_hi, b_hi) + _dot(a_hi, b_lo) + _dot(a_lo, b_hi)


def _mods_kernel(c_ref, w_ref, b_ref, o_ref):
    c = c_ref[...]
    s = c * jax.nn.sigmoid(c)
    s_hi, s_lo = _split_bf16(s)
    w_hi, w_lo = _split_bf16(w_ref[...])
    o_ref[...] = _dot_split(s_hi, s_lo, w_hi, w_lo) + b_ref[...]


def _mods(cond, w, b):
    rows, d = cond.shape
    n = w.shape[1]
    tn = min(n, 1024)
    return pl.pallas_call(
        _mods_kernel,
        out_shape=jax.ShapeDtypeStruct((rows, n), F32),
        grid=(n // tn,),
        in_specs=[pl.BlockSpec((rows, d), lambda j: (0, 0)),
                  pl.BlockSpec((d, tn), lambda j: (0, j)),
                  pl.BlockSpec((1, tn), lambda j: (0, j))],
        out_specs=pl.BlockSpec((rows, tn), lambda j: (0, j)),
        compiler_params=_cparams("arbitrary"),
        name="mods",
    )(cond, w, b.reshape(1, n))


def _rope_tables(n_tokens, grid_w):
    n_freq = HEAD_QK_DIM // 4
    pos = jnp.arange(n_tokens, dtype=jnp.int32)
    row = (pos // grid_w).astype(F32)
    col = (pos % grid_w).astype(F32)
    inv = ROPE_BASE ** (-jnp.arange(n_freq, dtype=F32) / n_freq)
    ar = row[:, None] * inv
    ac = col[:, None] * inv
    cos64 = jnp.concatenate([jnp.cos(ar), jnp.cos(ar), jnp.cos(ac), jnp.cos(ac)], axis=1)
    sin64 = jnp.concatenate([-jnp.sin(ar), jnp.sin(ar), -jnp.sin(ac), jnp.sin(ac)], axis=1)
    reps = HEAD_V_DIM // HEAD_QK_DIM
    return jnp.tile(cos64, (1, reps)), jnp.tile(sin64, (1, reps))


def _inproj_kernel(x_ref, sh_ref, sc_ref, w_ref, cos_ref, sin_ref,
                   u_ref, g_ref, q_ref, k_ref, v_ref, *, lru_w, n_heads, rope):
    h = _layer_norm(x_ref[0]) * (1.0 + sc_ref[0]) + sh_ref[0]
    p = _dot(h.astype(BF16), w_ref[...])
    u_ref[0] = p[:, :lru_w]
    g_ref[0] = p[:, lru_w:2 * lru_w].astype(BF16)
    qk_w = n_heads * HEAD_V_DIM
    base_q = 2 * lru_w
    base_k = base_q + qk_w
    base_v = base_k + qk_w
    if rope:
        cos = cos_ref[...]
        sin = sin_ref[...]
        lane = lax.broadcasted_iota(jnp.int32, cos.shape, 1)
        first = (lane % 32) < 16
    q_scale = HEAD_QK_DIM ** -0.5 * math.log2(math.e)
    for hd in range(n_heads):
        lo, hi = hd * HEAD_V_DIM, (hd + 1) * HEAD_V_DIM
        for base, ref, scale in ((base_q, q_ref, q_scale), (base_k, k_ref, None)):
            t = p[:, base + lo:base + hi]
            if rope:
                partner = jnp.where(first, pltpu.roll(t, LANES - 16, 1), pltpu.roll(t, 16, 1))
                t = t * cos + partner * sin
            if scale is not None:
                t = t * scale
            ref[0, hd] = t.astype(BF16)
        v_ref[0, hd] = p[:, base_v + lo:base_v + hi].astype(BF16)


def _inproj(x, shift, scale, w_in, cos, sin, *, lru_w, n_heads, rope):
    bsz, seq, d = x.shape
    in_w = w_in.shape[1]
    tm = min(seq, 512)
    kern = functools.partial(_inproj_kernel, lru_w=lru_w, n_heads=n_heads, rope=rope)
    head_shape = jax.ShapeDtypeStruct((bsz, n_heads, seq, HEAD_V_DIM), BF16)
    head_spec = pl.BlockSpec((1, n_heads, tm, HEAD_V_DIM), lambda b, i: (b, 0, i, 0))
    vec_spec = pl.BlockSpec((1, 1, d), lambda b, i: (b, 0, 0))
    return pl.pallas_call(
        kern,
        out_shape=(jax.ShapeDtypeStruct((bsz, seq, lru_w), F32),
                   jax.ShapeDtypeStruct((bsz, seq, lru_w), BF16),
                   head_shape, head_shape, head_shape),
        grid=(bsz, seq // tm),
        in_specs=[pl.BlockSpec((1, tm, d), lambda b, i: (b, i, 0)),
                  vec_spec, vec_spec,
                  pl.BlockSpec((d, in_w), lambda b, i: (0, 0)),
                  pl.BlockSpec((tm, HEAD_V_DIM), lambda b, i: (i, 0)),
                  pl.BlockSpec((tm, HEAD_V_DIM), lambda b, i: (i, 0))],
        out_specs=(pl.BlockSpec((1, tm, lru_w), lambda b, i: (b, i, 0)),
                   pl.BlockSpec((1, tm, lru_w), lambda b, i: (b, i, 0)),
                   head_spec, head_spec, head_spec),
        compiler_params=_cparams("parallel", "arbitrary"),
        name="inproj_rope" if rope else "inproj_ctx",
    )(x, shift, scale, w_in, cos, sin)


def _scan_rows(a, b, reverse):
    row = lax.broadcasted_iota(jnp.int32, a.shape, 0)
    for s in (1, 2, 4):
        shift = SUBLANES - s if reverse else s
        a_sh = pltpu.roll(a, shift, 0)
        b_sh = pltpu.roll(b, shift, 0)
        valid = (row < SUBLANES - s) if reverse else (row >= s)
        b = jnp.where(valid, a * b_sh + b, b)
        a = jnp.where(valid, a * a_sh, a)
    return a, b


def _lru_kernel(*refs, reverse, combine, n_chunks, chunk):
    if combine:
        (u_ref, up_ref, un_ref, cw_ref, cb_ref, wg_ref, bg_ref, lam_ref, h0_ref, hb_ref, g_ref,
         out_ref, hn_ref, carry_ref, a_scr, b_scr) = refs
    else:
        (u_ref, up_ref, un_ref, cw_ref, cb_ref, wg_ref, bg_ref, lam_ref, h0_ref,
         out_ref, hn_ref, carry_ref, a_scr, b_scr) = refs
    c = pl.program_id(1)
    cc = (n_chunks - 1 - c) if reverse else c
    width = u_ref.shape[-1]

    @pl.when(c == 0)
    def _():
        carry_ref[...] = h0_ref[0]

    u = u_ref[0]
    prev = jnp.where(cc > 0, up_ref[0], 0.0)
    nxt = jnp.where(cc < n_chunks - 1, un_ref[0], 0.0)
    row = lax.broadcasted_iota(jnp.int32, u.shape, 0)
    u_m1 = jnp.where(row == 0, prev[7:8], pltpu.roll(u, 1, 0))
    u_m2 = jnp.where(row == 0, prev[6:7], jnp.where(row == 1, prev[7:8], pltpu.roll(u, 2, 0)))
    u_p1 = jnp.where(row == chunk - 1, nxt[0:1], pltpu.roll(u, chunk - 1, 0))
    cw = cw_ref[...]
    xc = cb_ref[...] + cw[0:1] * u_m2 + cw[1:2] * u_m1 + cw[2:3] * u + cw[3:4] * u_p1

    z = _dot(xc.astype(BF16), wg_ref[...]) + bg_ref[...]
    r = jax.nn.sigmoid(z[:, :width])
    i = jax.nn.sigmoid(z[:, width:])
    nlam = -lam_ref[...]
    softplus = jnp.maximum(nlam, 0.0) + jnp.log(1.0 + jnp.exp(-jnp.abs(nlam)))
    log_a = -LRU_C * r * softplus
    a = jnp.exp(log_a)
    a_scr[...] = a
    b_scr[...] = jnp.sqrt(1.0 - jnp.exp(2.0 * log_a)) * (i * xc)

    n_groups = chunk // SUBLANES

    def body(j, carry):
        jj = (n_groups - 1 - j) if reverse else j
        r0 = pl.multiple_of(jj * SUBLANES, SUBLANES)
        a_cum, h_loc = _scan_rows(a_scr[pl.ds(r0, SUBLANES), :], b_scr[pl.ds(r0, SUBLANES), :], reverse)
        h = a_cum * carry + h_loc
        b_scr[pl.ds(r0, SUBLANES), :] = h
        return h[0:1] if reverse else h[SUBLANES - 1:SUBLANES]

    carry = lax.fori_loop(0, n_groups, body, carry_ref[...])
    carry_ref[...] = carry
    hn_ref[0] = carry
    h_all = b_scr[...]
    if combine:
        out_ref[0] = (jax.nn.gelu(g_ref[0].astype(F32)) * (h_all + hb_ref[0])).astype(out_ref.dtype)
    else:
        out_ref[0] = h_all


def _lru(u, conv_w, conv_b, w_gate, b_gate, lam, h0, *, reverse, other=None, gate=None):
    bsz, seq, width = u.shape
    chunk = min(seq, 512)
    n_chunks = seq // chunk
    halo_per_chunk = chunk // SUBLANES
    n_halo = seq // SUBLANES
    combine = other is not None

    def pos(c):
        return (n_chunks - 1 - c) if reverse else c

    tile = lambda b, c: (b, pos(c), 0)
    prev_halo = lambda b, c: (b, jnp.maximum(pos(c) * halo_per_chunk - 1, 0), 0)
    next_halo = lambda b, c: (b, jnp.minimum((pos(c) + 1) * halo_per_chunk, n_halo - 1), 0)
    const2 = lambda b, c: (0, 0)
    state = lambda b, c: (b, 0, 0)
    in_specs = [pl.BlockSpec((1, chunk, width), tile),
                pl.BlockSpec((1, SUBLANES, width), prev_halo),
                pl.BlockSpec((1, SUBLANES, width), next_halo),
                pl.BlockSpec((CONV_W, width), const2),
                pl.BlockSpec((1, width), const2),
                pl.BlockSpec((width, 2 * width), const2),
                pl.BlockSpec((1, 2 * width), const2),
                pl.BlockSpec((1, width), const2),
                pl.BlockSpec((1, 1, width), state)]
    args = [u, u, u, conv_w, conv_b, w_gate, b_gate, lam, h0]
    if combine:
        in_specs += [pl.BlockSpec((1, chunk, width), tile), pl.BlockSpec((1, chunk, width), tile)]
        args += [other, gate]
    kern = functools.partial(_lru_kernel, reverse=reverse, combine=combine,
                             n_chunks=n_chunks, chunk=chunk)
    return pl.pallas_call(
        kern,
        out_shape=(jax.ShapeDtypeStruct((bsz, seq, width), BF16 if combine else F32),
                   jax.ShapeDtypeStruct((bsz, 1, width), F32)),
        grid=(bsz, n_chunks),
        in_specs=in_specs,
        out_specs=(pl.BlockSpec((1, chunk, width), tile), pl.BlockSpec((1, 1, width), state)),
        scratch_shapes=[pltpu.VMEM((1, width), F32),
                        pltpu.VMEM((chunk, width), F32),
                        pltpu.VMEM((chunk, width), F32)],
        compiler_params=_cparams("parallel", "arbitrary"),
        name=("lru_bwd" if reverse else "lru_fwd") + ("_mix" if combine else ""),
    )(*args)


def _expand_block_diag(w):
    n, k, _ = w.shape
    eye = jnp.eye(n, dtype=w.dtype)
    return (eye[:, None, :, None] * w[:, :, None, :]).reshape(n * k, n * k)


def _attn_kernel(dl_ref, q_ref, k_ref, v_ref, gn_ref, o_ref,
                 q2_scr, m_scr, acc_scr, s_scr, p_scr, a_scr, *, tk, rb, unroll, n_valid, lam_init):
    tq = q_ref.shape[2]
    dv = v_ref.shape[3]
    q = q_ref[0, 0]
    lane = lax.broadcasted_iota(jnp.int32, q.shape, 1)
    zero = jnp.zeros_like(q)
    q2_scr[:tq, :] = jnp.where(lane < HEAD_QK_DIM, q, zero)
    q2_scr[tq:, :] = jnp.where(lane >= HEAD_QK_DIM, q, zero)
    m_scr[...] = jnp.full(m_scr.shape, -jnp.inf, F32)
    acc_scr[...] = jnp.zeros(acc_scr.shape, F32)

    def kv_chunk(ref, j):
        start = j * tk if isinstance(j, int) else pl.multiple_of(j * tk, tk)
        return ref[0, 0, pl.ds(start, tk), :]

    def scores(j):
        return lax.dot_general(q2_scr[...], kv_chunk(k_ref, j), (((1,), (1,)), ((), ())),
                               preferred_element_type=F32)

    def softmax_rows(s_ref, p_ref, a_ref):
        for r in range(2 * tq // rb):
            rows = pl.ds(r * rb, rb)
            s = s_ref[rows, :]
            m_prev = m_scr[rows, :]
            m_new = jnp.maximum(m_prev, jnp.max(s, axis=-1, keepdims=True))
            p_ref[rows, :] = jnp.exp2(s - m_new).astype(BF16)
            a_ref[rows, :] = jnp.exp2(m_prev - m_new)
            m_scr[rows, :] = m_new

    def accumulate(buf, j):
        key = lax.broadcasted_iota(jnp.int32, (tk, dv), 0) + j * tk
        col = lax.broadcasted_iota(jnp.int32, (tk, dv), 1)
        valid = jnp.where((col == 0) & (key < n_valid), 1.0, 0.0).astype(BF16)
        v_ext = jnp.concatenate([kv_chunk(v_ref, j), valid], axis=1)
        acc_scr[...] = a_scr[buf] * acc_scr[...] + _dot(p_scr[buf], v_ext)

    n_chunks = k_ref.shape[2] // tk
    s_scr[0] = scores(0)
    p_scr[1] = jnp.zeros(p_scr.shape[1:], BF16)
    a_scr[1] = jnp.ones(a_scr.shape[1:], F32)

    def step(j, cur, with_scores):
        if with_scores:
            s_scr[1 - cur] = scores(j + 1)
        prev = max(j - 1, 0) if isinstance(j, int) else jnp.maximum(j - 1, 0)
        accumulate(1 - cur, prev)
        softmax_rows(s_scr.at[cur], p_scr.at[cur], a_scr.at[cur])

    def body(jj, carry):
        for sub in range(unroll):
            step(unroll * jj + sub, sub % 2, True)
        return carry

    n_loops = (n_chunks - 1) // unroll
    lax.fori_loop(0, n_loops, body, 0)
    for j in range(unroll * n_loops, n_chunks):
        step(j, j % 2, j + 1 < n_chunks)
    accumulate((n_chunks - 1) % 2, n_chunks - 1)

    lp = dl_ref[...]
    lam = (jnp.exp(jnp.sum(lp[0:1] * lp[1:2], axis=-1, keepdims=True))
           - jnp.exp(jnp.sum(lp[2:3] * lp[3:4], axis=-1, keepdims=True)) + lam_init)
    acc = acc_scr[...]
    o_all = acc[:, :dv] / acc[:, dv:dv + 1]
    o = o_all[:tq] - lam * o_all[tq:]
    y = o * lax.rsqrt(jnp.mean(o * o, axis=-1, keepdims=True) + LN_EPS) * (1.0 - lam_init)
    o_ref[0] = (y * gn_ref[0]).astype(o_ref.dtype)


def _attention(diff_lambda, q, k, v, k_c, v_c, norm_g, *, lam_init):
    bsz, n_heads, seq, dv = q.shape
    n_valid = seq + k_c.shape[2]
    tq = min(seq, 512)
    tk = min(seq // 2, 512)
    rb = min(2 * tq, 32)
    pad = -n_valid % tk
    k_all = jnp.concatenate([k, k_c, jnp.broadcast_to(k_c[:, :, :1], (bsz, n_heads, pad, dv))], axis=2)
    v_all = jnp.concatenate([v, v_c, jnp.zeros((bsz, n_heads, pad, dv), v.dtype)], axis=2)
    unroll = 2
    kern = functools.partial(_attn_kernel, tk=tk, rb=rb, unroll=unroll, n_valid=n_valid,
                             lam_init=lam_init)
    kv_spec = pl.BlockSpec((1, 1, n_valid + pad, dv), lambda b, h, i: (b, h, 0, 0))
    return pl.pallas_call(
        kern,
        out_shape=jax.ShapeDtypeStruct((bsz, seq, n_heads * dv), BF16),
        grid=(bsz, n_heads, seq // tq),
        in_specs=[pl.BlockSpec(diff_lambda.shape, lambda b, h, i: (0, 0)),
                  pl.BlockSpec((1, 1, tq, dv), lambda b, h, i: (b, h, i, 0)),
                  kv_spec, kv_spec,
                  pl.BlockSpec((1, 1, dv), lambda b, h, i: (h, 0, 0))],
        out_specs=pl.BlockSpec((1, tq, dv), lambda b, h, i: (b, i, h)),
        scratch_shapes=[pltpu.VMEM((2 * tq, dv), BF16),
                        pltpu.VMEM((2 * tq, 1), F32),
                        pltpu.VMEM((2 * tq, 2 * dv), F32),
                        pltpu.VMEM((2, 2 * tq, tk), F32),
                        pltpu.VMEM((2, 2 * tq, tk), BF16),
                        pltpu.VMEM((2, 2 * tq, 1), F32)],
        compiler_params=_cparams("parallel", "parallel", "arbitrary"),
        name="diff_attn",
    )(diff_lambda, q, k_all, v_all, norm_g)


def _mid_kernel(x_ref, yl_ref, ya_ref, wo_ref, g1_ref, sh2_ref, sc2_ref, pg_ref, pb_ref,
                wrh_ref, wrl_ref, br_ref, xmid_ref, h2_ref, route_ref, counts_ref, cnt_scr,
                *, alpha, n_groups, per_group):
    @pl.when((pl.program_id(0) == 0) & (pl.program_id(1) == 0))
    def _():
        cnt_scr[...] = jnp.zeros(cnt_scr.shape, F32)

    lru_w = yl_ref.shape[-1]
    mix = _dot(yl_ref[0], wo_ref[:lru_w, :]) + _dot(ya_ref[0], wo_ref[lru_w:, :])
    x_mid = _layer_norm(alpha * x_ref[0] + g1_ref[0] * mix) * pg_ref[...] + pb_ref[...]
    xmid_ref[0] = x_mid
    h2 = _layer_norm(x_mid) * (1.0 + sc2_ref[0]) + sh2_ref[0]
    h2_ref[0] = h2

    h_hi, h_lo = _split_bf16(h2)
    logits = _dot_split(h_hi, h_lo, wrh_ref[...], wrl_ref[...]) + br_ref[...]
    lane = lax.broadcasted_iota(jnp.int32, logits.shape, 1)

    def first_argmax(vals, vmax):
        return jnp.min(jnp.where(vals == vmax, lane, LANES), axis=-1, keepdims=True)

    gl = jnp.where(lane < n_groups, logits, NEG_BIG)
    g_max = jnp.max(gl, axis=-1, keepdims=True)
    p_g = 1.0 / jnp.sum(jnp.exp(gl - g_max), axis=-1, keepdims=True)
    g_sel = first_argmax(gl, g_max)
    e_lo = n_groups + g_sel * per_group
    el = jnp.where((lane >= e_lo) & (lane < e_lo + per_group), logits, NEG_BIG)
    m1 = jnp.max(el, axis=-1, keepdims=True)
    i1 = first_argmax(el, m1)
    el2 = jnp.where(lane == i1, NEG_BIG, el)
    m2 = jnp.max(el2, axis=-1, keepdims=True)
    i2 = first_argmax(el2, m2)
    e2 = jnp.exp(m2 - m1)
    w1 = 1.0 / (1.0 + e2)
    w2 = e2 * w1

    ex1 = i1 - n_groups
    ex2 = i2 - n_groups
    oh1 = jnp.where(lane == ex1, 1.0, 0.0)
    oh2 = jnp.where(lane == ex2, 1.0, 0.0)
    both = oh1 + oh2
    tm = both.shape[0]
    tri = (lax.broadcasted_iota(jnp.int32, (tm, tm), 0)
           > lax.broadcasted_iota(jnp.int32, (tm, tm), 1)).astype(BF16)
    before = _dot(tri, both.astype(BF16)) + cnt_scr[...]
    rank1 = jnp.sum(oh1 * before, axis=-1, keepdims=True)
    rank2 = jnp.sum(oh2 * before, axis=-1, keepdims=True)
    cnt_scr[...] += jnp.sum(both, axis=0, keepdims=True)
    counts_ref[...] = cnt_scr[...]

    fields = (ex1.astype(F32), ex2.astype(F32), p_g * w1, p_g * w2, rank1, rank2)
    slab = jnp.zeros(logits.shape, F32)
    for idx, val in enumerate(fields):
        slab = jnp.where(lane == idx, val, slab)
    route_ref[0] = slab


def _mid(x, y_lru, y_att, w_out, g1, sh2, sc2, post_g, post_b, wr_hi, wr_lo, b_r,
         *, alpha, n_groups, per_group):
    bsz, seq, d = x.shape
    lru_w = y_lru.shape[-1]
    att_w = y_att.shape[-1]
    tm = min(seq, 512)
    kern = functools.partial(_mid_kernel, alpha=alpha, n_groups=n_groups, per_group=per_group)
    tok = lambda w: pl.BlockSpec((1, tm, w), lambda b, i: (b, i, 0))
    vec = pl.BlockSpec((1, 1, d), lambda b, i: (b, 0, 0))
    const = lambda r, c: pl.BlockSpec((r, c), lambda b, i: (0, 0))
    return pl.pallas_call(
        kern,
        out_shape=(jax.ShapeDtypeStruct((bsz, seq, d), F32),
                   jax.ShapeDtypeStruct((bsz, seq, d), F32),
                   jax.ShapeDtypeStruct((bsz, seq, LANES), F32),
                   jax.ShapeDtypeStruct((1, LANES), F32)),
        grid=(bsz, seq // tm),
        in_specs=[tok(d), tok(lru_w), tok(att_w), const(lru_w + att_w, d), vec, vec, vec,
                  const(1, d), const(1, d), const(d, LANES), const(d, LANES), const(1, LANES)],
        out_specs=(tok(d), tok(d), tok(LANES), const(1, LANES)),
        scratch_shapes=[pltpu.VMEM((1, LANES), F32)],
        compiler_params=_cparams("arbitrary", "arbitrary"),
        name="mid_router",
    )(x, y_lru, y_att, w_out, g1, sh2, sc2, post_g, post_b, wr_hi, wr_lo, b_r)


def _row_copy(src, src_row, dst, dst_row, sem):
    return pltpu.make_async_copy(src.at[pl.ds(src_row, 1)], dst.at[pl.ds(dst_row, 1)], sem)


def _drain_rows(src, dst, sem, n):
    pltpu.make_async_copy(src.at[pl.ds(0, n)], dst.at[pl.ds(0, n)], sem).wait()


def _dispatch_kernel(slot_ref, h_ref, init_hbm, out_hbm, sem, *, tile):
    del init_hbm

    def issue(t, carry):
        for k in range(2):
            _row_copy(h_ref, t, out_hbm, slot_ref[0, 0, 2 * t + k], sem).start()
        return carry

    lax.fori_loop(0, tile, issue, 0)
    for _ in range(2):
        _drain_rows(h_ref, out_hbm, sem, tile)


def _dispatch(h2, slots, n_rows, *, tile):
    n, d = h2.shape
    n_steps = n // tile
    return pl.pallas_call(
        functools.partial(_dispatch_kernel, tile=tile),
        out_shape=jax.ShapeDtypeStruct((n_rows, d), F32),
        grid=(n_steps,),
        in_specs=[pl.BlockSpec((1, 1, 2 * tile), lambda i: (i, 0, 0), memory_space=pltpu.SMEM),
                  pl.BlockSpec((tile, d), lambda i: (i, 0)),
                  pl.BlockSpec(memory_space=pl.ANY)],
        out_specs=pl.BlockSpec(memory_space=pl.ANY),
        scratch_shapes=[pltpu.SemaphoreType.DMA(())],
        input_output_aliases={2: 0},
        compiler_params=_cparams("arbitrary"),
        name="moe_dispatch",
    )(slots.reshape(n_steps, 1, 2 * tile), h2, jnp.zeros((n_rows, d), F32))


def _experts_kernel(te_ref, nu_ref, h_ref, w1_ref, w3_ref, w2_ref, y_ref):
    del te_ref
    used = pl.program_id(0) < nu_ref[0]

    @pl.when(used)
    def _():
        h = h_ref[...].astype(BF16)
        a = _dot(h, w1_ref[0])
        act = a * jax.nn.sigmoid(a) * _dot(h, w3_ref[0])
        y_ref[...] = _dot(act.astype(BF16), w2_ref[0])

    @pl.when(jnp.logical_not(used))
    def _():
        y_ref[...] = jnp.zeros(y_ref.shape, F32)


def _experts(hs, tile_expert, n_used, w1, w3, w2, *, tile):
    n_rows, d = hs.shape
    d_e = w1.shape[2]
    used = lambda i, te, nu: jnp.minimum(i, nu[0] - 1)
    row_spec = pl.BlockSpec((tile, d), lambda i, te, nu: (used(i, te, nu), 0))
    w_spec = lambda r, c: pl.BlockSpec((1, r, c), lambda i, te, nu: (te[used(i, te, nu)], 0, 0))
    return pl.pallas_call(
        _experts_kernel,
        out_shape=jax.ShapeDtypeStruct((n_rows, d), F32),
        grid_spec=pltpu.PrefetchScalarGridSpec(
            num_scalar_prefetch=2,
            grid=(n_rows // tile,),
            in_specs=[row_spec, w_spec(d, d_e), w_spec(d, d_e), w_spec(d_e, d)],
            out_specs=pl.BlockSpec((tile, d), lambda i, te, nu: (i, 0))),
        compiler_params=_cparams("arbitrary"),
        name="moe_experts",
    )(tile_expert, n_used, hs, w1, w3, w2)


def _combine_kernel(slot_ref, route_ref, y_hbm, xmid_ref, g2_ref, pg_ref, pb_ref, out_ref,
                    ya_scr, yb_scr, sem, *, alpha, tile):
    def issue(t, carry):
        _row_copy(y_hbm, slot_ref[0, 0, 2 * t], ya_scr, t, sem).start()
        _row_copy(y_hbm, slot_ref[0, 0, 2 * t + 1], yb_scr, t, sem).start()
        return carry

    lax.fori_loop(0, tile, issue, 0)
    _drain_rows(y_hbm, ya_scr, sem, tile)
    _drain_rows(y_hbm, yb_scr, sem, tile)
    route = route_ref[...]
    y = route[:, 2:3] * ya_scr[...] + route[:, 3:4] * yb_scr[...]
    z = alpha * xmid_ref[...] + g2_ref[0] * y
    out_ref[...] = _layer_norm(z) * pg_ref[...] + pb_ref[...]


def _combine(ys, slots, route, x_mid, g2, post_g, post_b, *, alpha, seq, tile):
    n, d = x_mid.shape
    n_steps = n // tile
    assert seq % tile == 0
    tok = lambda w: pl.BlockSpec((tile, w), lambda i: (i, 0))
    const = pl.BlockSpec((1, d), lambda i: (0, 0))
    return pl.pallas_call(
        functools.partial(_combine_kernel, alpha=alpha, tile=tile),
        out_shape=jax.ShapeDtypeStruct((n, d), F32),
        grid=(n_steps,),
        in_specs=[pl.BlockSpec((1, 1, 2 * tile), lambda i: (i, 0, 0), memory_space=pltpu.SMEM),
                  tok(LANES),
                  pl.BlockSpec(memory_space=pl.ANY),
                  tok(d),
                  pl.BlockSpec((1, 1, d), lambda i: (i * tile // seq, 0, 0)),
                  const, const],
        out_specs=tok(d),
        scratch_shapes=[pltpu.VMEM((tile, d), F32), pltpu.VMEM((tile, d), F32),
                        pltpu.SemaphoreType.DMA(())],
        compiler_params=_cparams("arbitrary"),
        name="moe_combine",
    )(slots.reshape(n_steps, 1, 2 * tile), route, ys, x_mid, g2, post_g, post_b)


def _moe(h2, route, counts, w1, w3, w2, x_mid, g2, post_g, post_b, *, alpha):
    bsz, seq, d = x_mid.shape
    n_experts = w1.shape[0]
    n = bsz * seq
    tile_e = 256
    tile_t = min(seq, 256)
    n_tiles = 2 * n // tile_e + n_experts
    route = route.reshape(n, LANES)
    expert = route[:, 0:2].astype(jnp.int32)
    rank = route[:, 4:6].astype(jnp.int32)
    cnt = counts[0, :n_experts].astype(jnp.int32)
    tiles = (cnt + tile_e - 1) // tile_e
    tile_end = jnp.cumsum(tiles)
    slots = ((tile_end - tiles) * tile_e)[expert] + rank
    tile_ids = jnp.arange(n_tiles, dtype=jnp.int32)
    tile_expert = jnp.minimum(jnp.sum((tile_end[None, :] <= tile_ids[:, None]).astype(jnp.int32), axis=1),
                              n_experts - 1)
    n_used = tile_end[-1:].astype(jnp.int32)
    hs = _dispatch(h2.reshape(n, d), slots, n_tiles * tile_e, tile=tile_t)
    ys = _experts(hs, tile_expert, n_used, w1, w3, w2, tile=tile_e)
    out = _combine(ys, slots, route, x_mid.reshape(n, d), g2, post_g, post_b,
                   alpha=alpha, seq=seq, tile=tile_t)
    return out.reshape(bsz, seq, d)


def _block(x, c, ctx, c_ctx, w_mod, b_mod, w_in, conv_w, conv_b, lru_wa, lru_ba, lru_wi, lru_bi,
           lru_lambda, diff_lambda, attn_norm_g, w_out, post_g, post_b, router_g_w, router_g_b,
           router_e_w, router_e_b, exp_w1, exp_w3, exp_w2, *, grid_w):
    depth = w_mod.shape[0]
    assert depth == 1, "single-layer block only"
    bsz, seq, d = x.shape
    lru_w = conv_w.shape[-1]
    n_heads, dv = attn_norm_g.shape[1:]
    assert dv == HEAD_V_DIM and conv_w.shape[1] == CONV_W
    n_groups, _, per_group = router_e_w.shape[1:]
    assert n_groups + n_groups * per_group <= LANES
    alpha = (2.0 * depth) ** 0.25
    lam_init = 0.8 - 0.6 * math.exp(0.0)

    cond = jnp.concatenate([c, c_ctx[None], jnp.zeros((SUBLANES - 1 - bsz % SUBLANES, d), F32)], axis=0)
    mods = _mods(cond, w_mod[0], b_mod[0]).reshape(cond.shape[0], 6, 1, d)
    sh1, sc1, g1, sh2, sc2, g2 = (mods[:bsz, j] for j in range(6))
    sh1c, sc1c = (jnp.broadcast_to(mods[bsz:bsz + 1, j], (bsz, 1, d)) for j in range(2))

    w_in_b = w_in[0].astype(BF16)
    cos, sin = _rope_tables(seq, grid_w)
    u, gt, q, k, v = _inproj(x, sh1, sc1, w_in_b, cos, sin, lru_w=lru_w, n_heads=n_heads, rope=True)
    n_ctx = ctx.shape[1]
    u_c, _, _, k_c, v_c = _inproj(ctx, sh1c, sc1c, w_in_b, cos[:n_ctx], sin[:n_ctx],
                                  lru_w=lru_w, n_heads=n_heads, rope=False)

    zero_state = jnp.zeros((bsz, 1, lru_w), F32)
    cb = conv_b[0].reshape(1, lru_w)

    def gate_params(direction):
        w_gate = jnp.concatenate([_expand_block_diag(lru_wa[0, direction]),
                                  _expand_block_diag(lru_wi[0, direction])], axis=1).astype(BF16)
        b_gate = jnp.concatenate([lru_ba[0, direction], lru_bi[0, direction]]).reshape(1, 2 * lru_w)
        return w_gate, b_gate, lru_lambda[0, direction].reshape(1, lru_w)

    fwd_p, bwd_p = gate_params(0), gate_params(1)
    _, seed_f = _lru(u_c, conv_w[0], cb, *fwd_p, zero_state, reverse=False)
    _, seed_b = _lru(u_c, conv_w[0], cb, *bwd_p, zero_state, reverse=True)
    h_bwd, _ = _lru(u, conv_w[0], cb, *bwd_p, seed_b, reverse=True)
    y_lru, _ = _lru(u, conv_w[0], cb, *fwd_p, seed_f, reverse=False, other=h_bwd, gate=gt)

    y_att = _attention(diff_lambda[0], q, k, v, k_c, v_c, attn_norm_g[0].reshape(n_heads, 1, dv),
                       lam_init=lam_init)

    w_r = jnp.concatenate([router_g_w[0], jnp.moveaxis(router_e_w[0], 0, 1).reshape(d, -1)], axis=1)
    b_r = jnp.concatenate([router_g_b[0], router_e_b[0].reshape(-1)])
    pad = LANES - w_r.shape[1]
    w_r = jnp.pad(w_r, ((0, 0), (0, pad)))
    b_r = jnp.pad(b_r, (0, pad)).reshape(1, LANES)
    wr_hi, wr_lo = _split_bf16(w_r)

    x_mid, h2, route, counts = _mid(x, y_lru, y_att, w_out[0].astype(BF16), g1, sh2, sc2,
                                    post_g[0, 0].reshape(1, d), post_b[0, 0].reshape(1, d),
                                    wr_hi, wr_lo, b_r, alpha=alpha, n_groups=n_groups,
                                    per_group=per_group)
    return _moe(h2, route, counts, exp_w1[0].astype(BF16), exp_w3[0].astype(BF16),
                exp_w2[0].astype(BF16), x_mid, g2, post_g[0, 1].reshape(1, d),
                post_b[0, 1].reshape(1, d), alpha=alpha)


def kernel(x, c, ctx, c_ctx, w_mod, b_mod, w_in, conv_w, conv_b, lru_wa, lru_ba, lru_wi, lru_bi, lru_lambda, diff_lambda, attn_norm_g, w_out, post_g, post_b, router_g_w, router_g_b, router_e_w, router_e_b, exp_w1, exp_w3, exp_w2):
    return _block(x, c, ctx, c_ctx, w_mod, b_mod, w_in, conv_w, conv_b, lru_wa, lru_ba, lru_wi,
                  lru_bi, lru_lambda, diff_lambda, attn_norm_g, w_out, post_g, post_b, router_g_w,
                  router_g_b, router_e_w, router_e_b, exp_w1, exp_w3, exp_w2, grid_w=GRID_W)
```

```python
import functools
import math

import jax
import jax.numpy as jnp
from jax import lax
from jax.experimental import pallas as pl
from jax.experimental.pallas import tpu as pltpu

F32 = jnp.float32
BF16 = jnp.bfloat16

LN_EPS = 1e-5
LRU_C = 8.0
ROPE_BASE = 10000.0
GRID_W = 64
HEAD_V_DIM = 128
HEAD_QK_DIM = HEAD_V_DIM // 2
CONV_W = 4
SUBLANES = 8
LANES = 128
NEG_BIG = -1e30
VMEM_LIMIT_BYTES = 56 * 1024 * 1024
ISSUE_UNROLL = 8


def _cparams(*sem):
    return pltpu.CompilerParams(dimension_semantics=sem, vmem_limit_bytes=VMEM_LIMIT_BYTES)


def _layer_norm(x):
    mu = jnp.mean(x, axis=-1, keepdims=True)
    xc = x - mu
    var = jnp.mean(xc * xc, axis=-1, keepdims=True)
    return xc * lax.rsqrt(var + LN_EPS)


def _sigmoid(x):
    return 0.5 * jnp.tanh(0.5 * x) + 0.5


def _split_bf16(x):
    hi = x.astype(BF16)
    lo = (x - hi.astype(F32)).astype(BF16)
    return hi, lo


def _dot(a, b):
    return jnp.dot(a, b, preferred_element_type=F32)


def _dot_split(a_hi, a_lo, b_hi, b_lo):
    return _dot(a_hi, b_hi) + _dot(a_hi, b_lo) + _dot(a_lo, b_hi)


def _mods_kernel(c_ref, w_ref, b_ref, o_ref):
    c = c_ref[...]
    s = c * jax.nn.sigmoid(c)
    s_hi, s_lo = _split_bf16(s)
    w_hi, w_lo = _split_bf16(w_ref[...])
    o_ref[...] = _dot_split(s_hi, s_lo, w_hi, w_lo) + b_ref[...]


def _mods(cond, w, b):
    rows, d = cond.shape
    n = w.shape[1]
    tn = min(n, 1024)
    return pl.pallas_call(
        _mods_kernel,
        out_shape=jax.ShapeDtypeStruct((rows, n), F32),
        grid=(n // tn,),
        in_specs=[pl.BlockSpec((rows, d), lambda j: (0, 0)),
                  pl.BlockSpec((d, tn), lambda j: (0, j)),
                  pl.BlockSpec((1, tn), lambda j: (0, j))],
        out_specs=pl.BlockSpec((rows, tn), lambda j: (0, j)),
        compiler_params=_cparams("arbitrary"),
        name="mods",
    )(cond, w, b.reshape(1, n))


def _rope_tables(n_tokens, grid_w):
    n_freq = HEAD_QK_DIM // 4
    pos = jnp.arange(n_tokens, dtype=jnp.int32)
    row = (pos // grid_w).astype(F32)
    col = (pos % grid_w).astype(F32)
    inv = ROPE_BASE ** (-jnp.arange(n_freq, dtype=F32) / n_freq)
    ar = row[:, None] * inv
    ac = col[:, None] * inv
    cos64 = jnp.concatenate([jnp.cos(ar), jnp.cos(ar), jnp.cos(ac), jnp.cos(ac)], axis=1)
    sin64 = jnp.concatenate([-jnp.sin(ar), jnp.sin(ar), -jnp.sin(ac), jnp.sin(ac)], axis=1)
    reps = HEAD_V_DIM // HEAD_QK_DIM
    return jnp.tile(cos64, (1, reps)), jnp.tile(sin64, (1, reps))


def _inproj_kernel(x_ref, sh_ref, sc_ref, w_ref, cos_ref, sin_ref,
                   u_ref, g_ref, q_ref, k_ref, v_ref, *, lru_w, n_heads, rope):
    h = _layer_norm(x_ref[0]) * (1.0 + sc_ref[0]) + sh_ref[0]
    p = _dot(h.astype(BF16), w_ref[...])
    u_ref[0] = p[:, :lru_w]
    g_ref[0] = p[:, lru_w:2 * lru_w].astype(BF16)
    qk_w = n_heads * HEAD_V_DIM
    base_q = 2 * lru_w
    base_k = base_q + qk_w
    base_v = base_k + qk_w
    if rope:
        cos = cos_ref[...]
        sin = sin_ref[...]
        lane = lax.broadcasted_iota(jnp.int32, cos.shape, 1)
        first = (lane % 32) < 16
    q_scale = HEAD_QK_DIM ** -0.5 * math.log2(math.e)
    for hd in range(n_heads):
        lo, hi = hd * HEAD_V_DIM, (hd + 1) * HEAD_V_DIM
        for base, ref, scale in ((base_q, q_ref, q_scale), (base_k, k_ref, None)):
            t = p[:, base + lo:base + hi]
            if rope:
                partner = jnp.where(first, pltpu.roll(t, LANES - 16, 1), pltpu.roll(t, 16, 1))
                t = t * cos + partner * sin
            if scale is not None:
                t = t * scale
            ref[0, hd] = t.astype(BF16)
        v_ref[0, hd] = p[:, base_v + lo:base_v + hi].astype(BF16)


def _inproj(x, shift, scale, w_in, cos, sin, *, lru_w, n_heads, rope):
    bsz, seq, d = x.shape
    in_w = w_in.shape[1]
    tm = min(seq, 512)
    kern = functools.partial(_inproj_kernel, lru_w=lru_w, n_heads=n_heads, rope=rope)
    head_shape = jax.ShapeDtypeStruct((bsz, n_heads, seq, HEAD_V_DIM), BF16)
    head_spec = pl.BlockSpec((1, n_heads, tm, HEAD_V_DIM), lambda b, i: (b, 0, i, 0))
    vec_spec = pl.BlockSpec((1, 1, d), lambda b, i: (b, 0, 0))
    return pl.pallas_call(
        kern,
        out_shape=(jax.ShapeDtypeStruct((bsz, seq, lru_w), F32),
                   jax.ShapeDtypeStruct((bsz, seq, lru_w), BF16),
                   head_shape, head_shape, head_shape),
        grid=(bsz, seq // tm),
        in_specs=[pl.BlockSpec((1, tm, d), lambda b, i: (b, i, 0)),
                  vec_spec, vec_spec,
                  pl.BlockSpec((d, in_w), lambda b, i: (0, 0)),
                  pl.BlockSpec((tm, HEAD_V_DIM), lambda b, i: (i, 0)),
                  pl.BlockSpec((tm, HEAD_V_DIM), lambda b, i: (i, 0))],
        out_specs=(pl.BlockSpec((1, tm, lru_w), lambda b, i: (b, i, 0)),
                   pl.BlockSpec((1, tm, lru_w), lambda b, i: (b, i, 0)),
                   head_spec, head_spec, head_spec),
        compiler_params=_cparams("parallel", "arbitrary"),
        name="inproj_rope" if rope else "inproj_ctx",
    )(x, shift, scale, w_in, cos, sin)


def _scan_rows(a, b, reverse):
    row = lax.broadcasted_iota(jnp.int32, a.shape, 0)
    for s in (1, 2, 4):
        shift = SUBLANES - s if reverse else s
        a_sh = pltpu.roll(a, shift, 0)
        b_sh = pltpu.roll(b, shift, 0)
        valid = (row < SUBLANES - s) if reverse else (row >= s)
        b = jnp.where(valid, a * b_sh + b, b)
        a = jnp.where(valid, a * a_sh, a)
    return a, b


def _lru_kernel(*refs, reverse, combine, n_chunks, chunk):
    if combine:
        (u_ref, up_ref, un_ref, cw_ref, cb_ref, wg_ref, bg_ref, lam_ref, h0_ref, hb_ref, g_ref,
         out_ref, hn_ref, carry_ref, a_scr, b_scr) = refs
    else:
        (u_ref, up_ref, un_ref, cw_ref, cb_ref, wg_ref, bg_ref, lam_ref, h0_ref,
         out_ref, hn_ref, carry_ref, a_scr, b_scr) = refs
    c = pl.program_id(1)
    cc = (n_chunks - 1 - c) if reverse else c
    width = u_ref.shape[-1]

    @pl.when(c == 0)
    def _():
        carry_ref[...] = h0_ref[0]

    u = u_ref[0]
    prev = jnp.where(cc > 0, up_ref[0], 0.0)
    nxt = jnp.where(cc < n_chunks - 1, un_ref[0], 0.0)
    row = lax.broadcasted_iota(jnp.int32, u.shape, 0)
    u_m1 = jnp.where(row == 0, prev[7:8], pltpu.roll(u, 1, 0))
    u_m2 = jnp.where(row == 0, prev[6:7], jnp.where(row == 1, prev[7:8], pltpu.roll(u, 2, 0)))
    u_p1 = jnp.where(row == chunk - 1, nxt[0:1], pltpu.roll(u, chunk - 1, 0))
    cw = cw_ref[...]
    xc = cb_ref[...] + cw[0:1] * u_m2 + cw[1:2] * u_m1 + cw[2:3] * u + cw[3:4] * u_p1

    z = _dot(xc.astype(BF16), wg_ref[...]) + bg_ref[...]
    r = _sigmoid(z[:, :width])
    i = _sigmoid(z[:, width:])
    nlam = -lam_ref[...]
    softplus = jnp.maximum(nlam, 0.0) + jnp.log(1.0 + jnp.exp(-jnp.abs(nlam)))
    a = jnp.exp(-LRU_C * r * softplus)
    a_scr[...] = a
    gap = 1.0 - a * a
    b_scr[...] = jnp.where(gap > 0.0, gap * lax.rsqrt(gap), 0.0) * (i * xc)

    n_groups = chunk // SUBLANES

    def body(j, carry):
        jj = (n_groups - 1 - j) if reverse else j
        r0 = pl.multiple_of(jj * SUBLANES, SUBLANES)
        a_cum, h_loc = _scan_rows(a_scr[pl.ds(r0, SUBLANES), :], b_scr[pl.ds(r0, SUBLANES), :], reverse)
        h = a_cum * carry + h_loc
        b_scr[pl.ds(r0, SUBLANES), :] = h
        return h[0:1] if reverse else h[SUBLANES - 1:SUBLANES]

    carry = lax.fori_loop(0, n_groups, body, carry_ref[...])
    carry_ref[...] = carry
    hn_ref[0] = carry
    h_all = b_scr[...]
    if combine:
        out_ref[0] = (jax.nn.gelu(g_ref[0].astype(F32)) * (h_all + hb_ref[0])).astype(out_ref.dtype)
    else:
        out_ref[0] = h_all


def _lru(u, conv_w, conv_b, w_gate, b_gate, lam, h0, *, reverse, other=None, gate=None):
    bsz, seq, width = u.shape
    chunk = min(seq, 512)
    n_chunks = seq // chunk
    halo_per_chunk = chunk // SUBLANES
    n_halo = seq // SUBLANES
    combine = other is not None

    def pos(c):
        return (n_chunks - 1 - c) if reverse else c

    tile = lambda b, c: (b, pos(c), 0)
    prev_halo = lambda b, c: (b, jnp.maximum(pos(c) * halo_per_chunk - 1, 0), 0)
    next_halo = lambda b, c: (b, jnp.minimum((pos(c) + 1) * halo_per_chunk, n_halo - 1), 0)
    const2 = lambda b, c: (0, 0)
    state = lambda b, c: (b, 0, 0)
    in_specs = [pl.BlockSpec((1, chunk, width), tile),
                pl.BlockSpec((1, SUBLANES, width), prev_halo),
                pl.BlockSpec((1, SUBLANES, width), next_halo),
                pl.BlockSpec((CONV_W, width), const2),
                pl.BlockSpec((1, width), const2),
                pl.BlockSpec((width, 2 * width), const2),
                pl.BlockSpec((1, 2 * width), const2),
                pl.BlockSpec((1, width), const2),
                pl.BlockSpec((1, 1, width), state)]
    args = [u, u, u, conv_w, conv_b, w_gate, b_gate, lam, h0]
    if combine:
        in_specs += [pl.BlockSpec((1, chunk, width), tile), pl.BlockSpec((1, chunk, width), tile)]
        args += [other, gate]
    kern = functools.partial(_lru_kernel, reverse=reverse, combine=combine,
                             n_chunks=n_chunks, chunk=chunk)
    return pl.pallas_call(
        kern,
        out_shape=(jax.ShapeDtypeStruct((bsz, seq, width), BF16 if combine else F32),
                   jax.ShapeDtypeStruct((bsz, 1, width), F32)),
        grid=(bsz, n_chunks),
        in_specs=in_specs,
        out_specs=(pl.BlockSpec((1, chunk, width), tile), pl.BlockSpec((1, 1, width), state)),
        scratch_shapes=[pltpu.VMEM((1, width), F32),
                        pltpu.VMEM((chunk, width), F32),
                        pltpu.VMEM((chunk, width), F32)],
        compiler_params=_cparams("parallel", "arbitrary"),
        name=("lru_bwd" if reverse else "lru_fwd") + ("_mix" if combine else ""),
    )(*args)


def _expand_block_diag(w):
    n, k, _ = w.shape
    eye = jnp.eye(n, dtype=w.dtype)
    return (eye[:, None, :, None] * w[:, :, None, :]).reshape(n * k, n * k)


def _attn_kernel(dl_ref, q_ref, k_ref, v_ref, gn_ref, o_ref,
                 q2_scr, m_scr, acc_scr, s_scr, p_scr, a_scr,
                 *, tk, rb, unroll, n_valid, lam_init):
    tq = q_ref.shape[2]
    dv = v_ref.shape[3]
    q = q_ref[0, 0]
    lane = lax.broadcasted_iota(jnp.int32, q.shape, 1)
    zero = jnp.zeros_like(q)
    q2_scr[:tq, :] = jnp.where(lane < HEAD_QK_DIM, q, zero)
    q2_scr[tq:, :] = jnp.where(lane >= HEAD_QK_DIM, q, zero)
    m_scr[...] = jnp.full(m_scr.shape, -jnp.inf, F32)
    acc_scr[...] = jnp.zeros(acc_scr.shape, F32)

    def kv_chunk(ref, j):
        start = j * tk if isinstance(j, int) else pl.multiple_of(j * tk, tk)
        return ref[0, 0, pl.ds(start, tk), :]

    def scores(j):
        return lax.dot_general(q2_scr[...], kv_chunk(k_ref, j), (((1,), (1,)), ((), ())),
                               preferred_element_type=F32)

    def softmax_rows(buf):
        for r in range(2 * tq // rb):
            rows = pl.ds(r * rb, rb)
            s = s_scr[buf, rows, :]
            m_prev = m_scr[rows, :]
            m_new = jnp.maximum(m_prev, jnp.max(s, axis=-1, keepdims=True))
            p_scr[buf, rows, :] = jnp.exp2(s - m_new).astype(BF16)
            a_scr[buf, rows, :] = jnp.exp2(m_prev - m_new)
            m_scr[rows, :] = m_new

    def accumulate(buf, j):
        key = lax.broadcasted_iota(jnp.int32, (tk, dv), 0) + j * tk
        col = lax.broadcasted_iota(jnp.int32, (tk, dv), 1)
        valid = jnp.where((col == 0) & (key < n_valid), 1.0, 0.0).astype(BF16)
        v_ext = jnp.concatenate([kv_chunk(v_ref, j), valid], axis=1)
        acc_scr[...] = a_scr[buf] * acc_scr[...] + _dot(p_scr[buf], v_ext)

    n_chunks = k_ref.shape[2] // tk
    s_scr[0] = scores(0)
    p_scr[1] = jnp.zeros(p_scr.shape[1:], BF16)
    a_scr[1] = jnp.ones(a_scr.shape[1:], F32)

    def step(j, cur, with_scores):
        if with_scores:
            s_scr[1 - cur] = scores(j + 1)
        prev = max(j - 1, 0) if isinstance(j, int) else jnp.maximum(j - 1, 0)
        accumulate(1 - cur, prev)
        softmax_rows(cur)

    def body(jj, carry):
        for sub in range(unroll):
            step(unroll * jj + sub, sub % 2, True)
        return carry

    n_loops = (n_chunks - 1) // unroll
    lax.fori_loop(0, n_loops, body, 0)
    for j in range(unroll * n_loops, n_chunks):
        step(j, j % 2, j + 1 < n_chunks)
    accumulate((n_chunks - 1) % 2, n_chunks - 1)

    lp = dl_ref[...]
    lam = (jnp.exp(jnp.sum(lp[0:1] * lp[1:2], axis=-1, keepdims=True))
           - jnp.exp(jnp.sum(lp[2:3] * lp[3:4], axis=-1, keepdims=True)) + lam_init)
    o_all = acc_scr[:, :dv] / acc_scr[:, dv:dv + 1]
    o = o_all[:tq] - lam * o_all[tq:]
    y = o * lax.rsqrt(jnp.mean(o * o, axis=-1, keepdims=True) + LN_EPS) * (1.0 - lam_init)
    o_ref[0] = (y * gn_ref[0]).astype(o_ref.dtype)


def _attention(diff_lambda, q, k, v, k_c, v_c, norm_g, *, lam_init):
    bsz, n_heads, seq, dv = q.shape
    n_valid = seq + k_c.shape[2]
    tq = min(seq, 512)
    tk = min(seq // 2, 512)
    rb = min(2 * tq, 32)
    pad = -n_valid % tk
    k_all = jnp.concatenate([k, k_c, jnp.broadcast_to(k_c[:, :, :1], (bsz, n_heads, pad, dv))], axis=2)
    v_all = jnp.concatenate([v, v_c, jnp.zeros((bsz, n_heads, pad, dv), v.dtype)], axis=2)
    unroll = 2
    kern = functools.partial(_attn_kernel, tk=tk, rb=rb, unroll=unroll, n_valid=n_valid,
                             lam_init=lam_init)
    kv_spec = pl.BlockSpec((1, 1, n_valid + pad, dv), lambda b, h, i: (b, h, 0, 0))
    return pl.pallas_call(
        kern,
        out_shape=jax.ShapeDtypeStruct((bsz, seq, n_heads * dv), BF16),
        grid=(bsz, n_heads, seq // tq),
        in_specs=[pl.BlockSpec(diff_lambda.shape, lambda b, h, i: (0, 0)),
                  pl.BlockSpec((1, 1, tq, dv), lambda b, h, i: (b, h, i, 0)),
                  kv_spec, kv_spec,
                  pl.BlockSpec((1, 1, dv), lambda b, h, i: (h, 0, 0))],
        out_specs=pl.BlockSpec((1, tq, dv), lambda b, h, i: (b, i, h)),
        scratch_shapes=[pltpu.VMEM((2 * tq, dv), BF16),
                        pltpu.VMEM((2 * tq, 1), F32),
                        pltpu.VMEM((2 * tq, 2 * dv), F32),
                        pltpu.VMEM((2, 2 * tq, tk), F32),
                        pltpu.VMEM((2, 2 * tq, tk), BF16),
                        pltpu.VMEM((2, 2 * tq, 1), F32)],
        compiler_params=_cparams("parallel", "parallel", "arbitrary"),
        name="diff_attn",
    )(diff_lambda, q, k_all, v_all, norm_g)


def _mid_kernel(x_ref, yl_ref, ya_ref, wo_ref, g1_ref, sh2_ref, sc2_ref, pg_ref, pb_ref,
                wrh_ref, wrl_ref, br_ref, xmid_ref, h2_ref, route_ref, counts_ref, cnt_scr,
                *, alpha, n_groups, per_group):
    @pl.when((pl.program_id(0) == 0) & (pl.program_id(1) == 0))
    def _():
        cnt_scr[...] = jnp.zeros(cnt_scr.shape, F32)

    lru_w = yl_ref.shape[-1]
    mix = _dot(yl_ref[0], wo_ref[:lru_w, :]) + _dot(ya_ref[0], wo_ref[lru_w:, :])
    x_mid = _layer_norm(alpha * x_ref[0] + g1_ref[0] * mix) * pg_ref[...] + pb_ref[...]
    xmid_ref[0] = x_mid
    h2 = _layer_norm(x_mid) * (1.0 + sc2_ref[0]) + sh2_ref[0]
    h2_ref[0] = h2

    h_hi, h_lo = _split_bf16(h2)
    logits = _dot_split(h_hi, h_lo, wrh_ref[...], wrl_ref[...]) + br_ref[...]
    lane = lax.broadcasted_iota(jnp.int32, logits.shape, 1)

    def first_argmax(vals, vmax):
        return jnp.min(jnp.where(vals == vmax, lane, LANES), axis=-1, keepdims=True)

    gl = jnp.where(lane < n_groups, logits, NEG_BIG)
    g_max = jnp.max(gl, axis=-1, keepdims=True)
    p_g = 1.0 / jnp.sum(jnp.exp(gl - g_max), axis=-1, keepdims=True)
    g_sel = first_argmax(gl, g_max)
    e_lo = n_groups + g_sel * per_group
    el = jnp.where((lane >= e_lo) & (lane < e_lo + per_group), logits, NEG_BIG)
    m1 = jnp.max(el, axis=-1, keepdims=True)
    i1 = first_argmax(el, m1)
    el2 = jnp.where(lane == i1, NEG_BIG, el)
    m2 = jnp.max(el2, axis=-1, keepdims=True)
    i2 = first_argmax(el2, m2)
    e2 = jnp.exp(m2 - m1)
    w1 = 1.0 / (1.0 + e2)
    w2 = e2 * w1

    ex1 = i1 - n_groups
    ex2 = i2 - n_groups
    oh1 = jnp.where(lane == ex1, 1.0, 0.0)
    oh2 = jnp.where(lane == ex2, 1.0, 0.0)
    both = oh1 + oh2
    tm = both.shape[0]
    tri = (lax.broadcasted_iota(jnp.int32, (tm, tm), 0)
           > lax.broadcasted_iota(jnp.int32, (tm, tm), 1)).astype(BF16)
    before = _dot(tri, both.astype(BF16)) + cnt_scr[...]
    rank1 = jnp.sum(oh1 * before, axis=-1, keepdims=True)
    rank2 = jnp.sum(oh2 * before, axis=-1, keepdims=True)
    cnt_scr[...] += jnp.sum(both, axis=0, keepdims=True)
    counts_ref[...] = cnt_scr[...]

    fields = (ex1.astype(F32), ex2.astype(F32), p_g * w1, p_g * w2, rank1, rank2)
    slab = jnp.zeros(logits.shape, F32)
    for idx, val in enumerate(fields):
        slab = jnp.where(lane == idx, val, slab)
    route_ref[0] = slab


def _mid(x, y_lru, y_att, w_out, g1, sh2, sc2, post_g, post_b, wr_hi, wr_lo, b_r,
         *, alpha, n_groups, per_group):
    bsz, seq, d = x.shape
    lru_w = y_lru.shape[-1]
    att_w = y_att.shape[-1]
    tm = min(seq, 512)
    kern = functools.partial(_mid_kernel, alpha=alpha, n_groups=n_groups, per_group=per_group)
    tok = lambda w: pl.BlockSpec((1, tm, w), lambda b, i: (b, i, 0))
    vec = pl.BlockSpec((1, 1, d), lambda b, i: (b, 0, 0))
    const = lambda r, c: pl.BlockSpec((r, c), lambda b, i: (0, 0))
    return pl.pallas_call(
        kern,
        out_shape=(jax.ShapeDtypeStruct((bsz, seq, d), F32),
                   jax.ShapeDtypeStruct((bsz, seq, d), F32),
                   jax.ShapeDtypeStruct((bsz, seq, LANES), F32),
                   jax.ShapeDtypeStruct((1, LANES), F32)),
        grid=(bsz, seq // tm),
        in_specs=[tok(d), tok(lru_w), tok(att_w), const(lru_w + att_w, d), vec, vec, vec,
                  const(1, d), const(1, d), const(d, LANES), const(d, LANES), const(1, LANES)],
        out_specs=(tok(d), tok(d), tok(LANES), const(1, LANES)),
        scratch_shapes=[pltpu.VMEM((1, LANES), F32)],
        compiler_params=_cparams("arbitrary", "arbitrary"),
        name="mid_router",
    )(x, y_lru, y_att, w_out, g1, sh2, sc2, post_g, post_b, wr_hi, wr_lo, b_r)


def _row_copy(src, src_row, dst, dst_row, sem):
    return pltpu.make_async_copy(src.at[pl.ds(src_row, 1)], dst.at[pl.ds(dst_row, 1)], sem)


def _drain_rows(src, dst, sem, n):
    pltpu.make_async_copy(src.at[pl.ds(0, n)], dst.at[pl.ds(0, n)], sem).wait()


def _dispatch_kernel(slot_ref, h_ref, init_hbm, out_hbm, sem, *, tile):
    del init_hbm

    def issue(t, carry):
        for k in range(2):
            _row_copy(h_ref, t, out_hbm, slot_ref[0, 0, 2 * t + k], sem).start()
        return carry

    lax.fori_loop(0, tile, issue, 0, unroll=ISSUE_UNROLL)
    for _ in range(2):
        _drain_rows(h_ref, out_hbm, sem, tile)


def _dispatch(h2, slots, n_rows, *, tile):
    n, d = h2.shape
    n_steps = n // tile
    return pl.pallas_call(
        functools.partial(_dispatch_kernel, tile=tile),
        out_shape=jax.ShapeDtypeStruct((n_rows, d), F32),
        grid=(n_steps,),
        in_specs=[pl.BlockSpec((1, 1, 2 * tile), lambda i: (i, 0, 0), memory_space=pltpu.SMEM),
                  pl.BlockSpec((tile, d), lambda i: (i, 0)),
                  pl.BlockSpec(memory_space=pl.ANY)],
        out_specs=pl.BlockSpec(memory_space=pl.ANY),
        scratch_shapes=[pltpu.SemaphoreType.DMA(())],
        input_output_aliases={2: 0},
        compiler_params=_cparams("arbitrary"),
        name="moe_dispatch",
    )(slots.reshape(n_steps, 1, 2 * tile), h2, jnp.zeros((n_rows, d), F32))


def _experts_kernel(te_ref, nu_ref, h_ref, w1_ref, w3_ref, w2_ref, y_ref):
    del te_ref
    used = pl.program_id(0) < nu_ref[0]

    @pl.when(used)
    def _():
        h = h_ref[...].astype(BF16)
        a = _dot(h, w1_ref[0].astype(BF16))
        act = a * _sigmoid(a) * _dot(h, w3_ref[0].astype(BF16))
        y_ref[...] = _dot(act.astype(BF16), w2_ref[0].astype(BF16))

    @pl.when(jnp.logical_not(used))
    def _():
        y_ref[...] = jnp.zeros(y_ref.shape, F32)


def _experts(hs, tile_expert, n_used, w1, w3, w2, *, tile):
    n_rows, d = hs.shape
    d_e = w1.shape[2]
    used = lambda i, te, nu: jnp.minimum(i, nu[0] - 1)
    row_spec = pl.BlockSpec((tile, d), lambda i, te, nu: (used(i, te, nu), 0))
    w_spec = lambda r, c: pl.BlockSpec((1, r, c), lambda i, te, nu: (te[used(i, te, nu)], 0, 0))
    return pl.pallas_call(
        _experts_kernel,
        out_shape=jax.ShapeDtypeStruct((n_rows, d), F32),
        grid_spec=pltpu.PrefetchScalarGridSpec(
            num_scalar_prefetch=2,
            grid=(n_rows // tile,),
            in_specs=[row_spec, w_spec(d, d_e), w_spec(d, d_e), w_spec(d_e, d)],
            out_specs=pl.BlockSpec((tile, d), lambda i, te, nu: (i, 0))),
        compiler_params=_cparams("arbitrary"),
        name="moe_experts",
    )(tile_expert, n_used, hs, w1, w3, w2)


def _combine_kernel(slot_ref, route_ref, y_hbm, xmid_ref, g2_ref, pg_ref, pb_ref, out_ref,
                    ya_scr, yb_scr, sem, *, alpha, tile):
    def issue(t, carry):
        _row_copy(y_hbm, slot_ref[0, 0, 2 * t], ya_scr, t, sem).start()
        _row_copy(y_hbm, slot_ref[0, 0, 2 * t + 1], yb_scr, t, sem).start()
        return carry

    lax.fori_loop(0, tile, issue, 0, unroll=ISSUE_UNROLL)
    _drain_rows(y_hbm, ya_scr, sem, tile)
    _drain_rows(y_hbm, yb_scr, sem, tile)
    route = route_ref[...]
    y = route[:, 2:3] * ya_scr[...] + route[:, 3:4] * yb_scr[...]
    z = alpha * xmid_ref[...] + g2_ref[0] * y
    out_ref[...] = _layer_norm(z) * pg_ref[...] + pb_ref[...]


def _combine(ys, slots, route, x_mid, g2, post_g, post_b, *, alpha, seq, tile):
    n, d = x_mid.shape
    n_steps = n // tile
    assert seq % tile == 0
    tok = lambda w: pl.BlockSpec((tile, w), lambda i: (i, 0))
    const = pl.BlockSpec((1, d), lambda i: (0, 0))
    return pl.pallas_call(
        functools.partial(_combine_kernel, alpha=alpha, tile=tile),
        out_shape=jax.ShapeDtypeStruct((n, d), F32),
        grid=(n_steps,),
        in_specs=[pl.BlockSpec((1, 1, 2 * tile), lambda i: (i, 0, 0), memory_space=pltpu.SMEM),
                  tok(LANES),
                  pl.BlockSpec(memory_space=pl.ANY),
                  tok(d),
                  pl.BlockSpec((1, 1, d), lambda i: (i * tile // seq, 0, 0)),
                  const, const],
        out_specs=tok(d),
        scratch_shapes=[pltpu.VMEM((tile, d), F32), pltpu.VMEM((tile, d), F32),
                        pltpu.SemaphoreType.DMA(())],
        compiler_params=_cparams("arbitrary"),
        name="moe_combine",
    )(slots.reshape(n_steps, 1, 2 * tile), route, ys, x_mid, g2, post_g, post_b)


def _moe(h2, route, counts, w1, w3, w2, x_mid, g2, post_g, post_b, *, alpha):
    bsz, seq, d = x_mid.shape
    n_experts = w1.shape[0]
    n = bsz * seq
    tile_e = 256
    tile_t = min(seq, 512)
    n_tiles = 2 * n // tile_e + n_experts
    route = route.reshape(n, LANES)
    expert = route[:, 0:2].astype(jnp.int32)
    rank = route[:, 4:6].astype(jnp.int32)
    cnt = counts[0, :n_experts].astype(jnp.int32)
    tiles = (cnt + tile_e - 1) // tile_e
    tile_end = jnp.cumsum(tiles)
    slots = ((tile_end - tiles) * tile_e)[expert] + rank
    tile_ids = jnp.arange(n_tiles, dtype=jnp.int32)
    tile_expert = jnp.minimum(jnp.sum((tile_end[None, :] <= tile_ids[:, None]).astype(jnp.int32), axis=1),
                              n_experts - 1)
    n_used = tile_end[-1:].astype(jnp.int32)
    hs = _dispatch(h2.reshape(n, d), slots, n_tiles * tile_e, tile=tile_t)
    ys = _experts(hs, tile_expert, n_used, w1, w3, w2, tile=tile_e)
    out = _combine(ys, slots, route, x_mid.reshape(n, d), g2, post_g, post_b,
                   alpha=alpha, seq=seq, tile=tile_t)
    return out.reshape(bsz, seq, d)


def _block(x, c, ctx, c_ctx, w_mod, b_mod, w_in, conv_w, conv_b, lru_wa, lru_ba, lru_wi, lru_bi,
           lru_lambda, diff_lambda, attn_norm_g, w_out, post_g, post_b, router_g_w, router_g_b,
           router_e_w, router_e_b, exp_w1, exp_w3, exp_w2, *, grid_w):
    depth = w_mod.shape[0]
    assert depth == 1, "single-layer block only"
    bsz, seq, d = x.shape
    lru_w = conv_w.shape[-1]
    n_heads, dv = attn_norm_g.shape[1:]
    assert dv == HEAD_V_DIM and conv_w.shape[1] == CONV_W
    n_groups, _, per_group = router_e_w.shape[1:]
    assert n_groups + n_groups * per_group <= LANES
    alpha = (2.0 * depth) ** 0.25
    lam_init = 0.8 - 0.6 * math.exp(0.0)

    cond = jnp.concatenate([c, c_ctx[None], jnp.zeros((SUBLANES - 1 - bsz % SUBLANES, d), F32)], axis=0)
    mods = _mods(cond, w_mod[0], b_mod[0]).reshape(cond.shape[0], 6, 1, d)
    sh1, sc1, g1, sh2, sc2, g2 = (mods[:bsz, j] for j in range(6))
    sh1c, sc1c = (jnp.broadcast_to(mods[bsz:bsz + 1, j], (bsz, 1, d)) for j in range(2))

    w_in_b = w_in[0].astype(BF16)
    cos, sin = _rope_tables(seq, grid_w)
    u, gt, q, k, v = _inproj(x, sh1, sc1, w_in_b, cos, sin, lru_w=lru_w, n_heads=n_heads, rope=True)
    n_ctx = ctx.shape[1]
    u_c, _, _, k_c, v_c = _inproj(ctx, sh1c, sc1c, w_in_b, cos[:n_ctx], sin[:n_ctx],
                                  lru_w=lru_w, n_heads=n_heads, rope=False)

    zero_state = jnp.zeros((bsz, 1, lru_w), F32)
    cb = conv_b[0].reshape(1, lru_w)

    def gate_params(direction):
        w_gate = jnp.concatenate([_expand_block_diag(lru_wa[0, direction]),
                                  _expand_block_diag(lru_wi[0, direction])], axis=1).astype(BF16)
        b_gate = jnp.concatenate([lru_ba[0, direction], lru_bi[0, direction]]).reshape(1, 2 * lru_w)
        return w_gate, b_gate, lru_lambda[0, direction].reshape(1, lru_w)

    fwd_p, bwd_p = gate_params(0), gate_params(1)
    _, seed_f = _lru(u_c, conv_w[0], cb, *fwd_p, zero_state, reverse=False)
    _, seed_b = _lru(u_c, conv_w[0], cb, *bwd_p, zero_state, reverse=True)
    h_bwd, _ = _lru(u, conv_w[0], cb, *bwd_p, seed_b, reverse=True)
    y_lru, _ = _lru(u, conv_w[0], cb, *fwd_p, seed_f, reverse=False, other=h_bwd, gate=gt)

    y_att = _attention(diff_lambda[0], q, k, v, k_c, v_c, attn_norm_g[0].reshape(n_heads, 1, dv),
                       lam_init=lam_init)

    w_r = jnp.concatenate([router_g_w[0], jnp.moveaxis(router_e_w[0], 0, 1).reshape(d, -1)], axis=1)
    b_r = jnp.concatenate([router_g_b[0], router_e_b[0].reshape(-1)])
    pad = LANES - w_r.shape[1]
    w_r = jnp.pad(w_r, ((0, 0), (0, pad)))
    b_r = jnp.pad(b_r, (0, pad)).reshape(1, LANES)
    wr_hi, wr_lo = _split_bf16(w_r)

    x_mid, h2, route, counts = _mid(x, y_lru, y_att, w_out[0].astype(BF16), g1, sh2, sc2,
                                    post_g[0, 0].reshape(1, d), post_b[0, 0].reshape(1, d),
                                    wr_hi, wr_lo, b_r, alpha=alpha, n_groups=n_groups,
                                    per_group=per_group)
    return _moe(h2, route, counts, exp_w1[0], exp_w3[0], exp_w2[0], x_mid, g2,
                post_g[0, 1].reshape(1, d),
                post_b[0, 1].reshape(1, d), alpha=alpha)


def kernel(x, c, ctx, c_ctx, w_mod, b_mod, w_in, conv_w, conv_b, lru_wa, lru_ba, lru_wi, lru_bi, lru_lambda, diff_lambda, attn_norm_g, w_out, post_g, post_b, router_g_w, router_g_b, router_e_w, router_e_b, exp_w1, exp_w3, exp_w2):
    return _block(x, c, ctx, c_ctx, w_mod, b_mod, w_in, conv_w, conv_b, lru_wa, lru_ba, lru_wi,
                  lru_bi, lru_lambda, diff_lambda, attn_norm_g, w_out, post_g, post_b, router_g_w,
                  router_g_b, router_e_w, router_e_b, exp_w1, exp_w3, exp_w2, grid_w=GRID_W)
```

```python
import functools
import math

import jax
import jax.numpy as jnp
from jax import lax
from jax.experimental import pallas as pl
from jax.experimental.pallas import tpu as pltpu

F32 = jnp.float32
BF16 = jnp.bfloat16

LN_EPS = 1e-5
LRU_C = 8.0
ROPE_BASE = 10000.0
GRID_W = 64
HEAD_V_DIM = 128
HEAD_QK_DIM = HEAD_V_DIM // 2
CONV_W = 4
SUBLANES = 8
LANES = 128
NEG_BIG = -1e30
VMEM_LIMIT_BYTES = 56 * 1024 * 1024
ISSUE_UNROLL = 8


def _cparams(*sem):
    return pltpu.CompilerParams(dimension_semantics=sem, vmem_limit_bytes=VMEM_LIMIT_BYTES)


def _layer_norm(x):
    mu = jnp.mean(x, axis=-1, keepdims=True)
    xc = x - mu
    var = jnp.mean(xc * xc, axis=-1, keepdims=True)
    return xc * lax.rsqrt(var + LN_EPS)


def _sigmoid(x):
    return 0.5 * jnp.tanh(0.5 * x) + 0.5


def _split_bf16(x):
    hi = x.astype(BF16)
    lo = (x - hi.astype(F32)).astype(BF16)
    return hi, lo


def _dot(a, b):
    return jnp.dot(a, b, preferred_element_type=F32)


def _dot_split(a_hi, a_lo, b_hi, b_lo):
    return _dot(a_hi, b_hi) + _dot(a_hi, b_lo) + _dot(a_lo, b_hi)


def _mods_kernel(c_ref, w_ref, b_ref, o_ref):
    c = c_ref[...]
    s = c * jax.nn.sigmoid(c)
    s_hi, s_lo = _split_bf16(s)
    w_hi, w_lo = _split_bf16(w_ref[...])
    o_ref[...] = _dot_split(s_hi, s_lo, w_hi, w_lo) + b_ref[...]


def _mods(cond, w, b):
    rows, d = cond.shape
    n = w.shape[1]
    tn = min(n, 1024)
    return pl.pallas_call(
        _mods_kernel,
        out_shape=jax.ShapeDtypeStruct((rows, n), F32),
        grid=(n // tn,),
        in_specs=[pl.BlockSpec((rows, d), lambda j: (0, 0)),
                  pl.BlockSpec((d, tn), lambda j: (0, j)),
                  pl.BlockSpec((1, tn), lambda j: (0, j))],
        out_specs=pl.BlockSpec((rows, tn), lambda j: (0, j)),
        compiler_params=_cparams("arbitrary"),
        name="mods",
    )(cond, w, b.reshape(1, n))


def _rope_tables(n_tokens, grid_w):
    n_freq = HEAD_QK_DIM // 4
    pos = jnp.arange(n_tokens, dtype=jnp.int32)
    row = (pos // grid_w).astype(F32)
    col = (pos % grid_w).astype(F32)
    inv = ROPE_BASE ** (-jnp.arange(n_freq, dtype=F32) / n_freq)
    ar = row[:, None] * inv
    ac = col[:, None] * inv
    cos64 = jnp.concatenate([jnp.cos(ar), jnp.cos(ar), jnp.cos(ac), jnp.cos(ac)], axis=1)
    sin64 = jnp.concatenate([-jnp.sin(ar), jnp.sin(ar), -jnp.sin(ac), jnp.sin(ac)], axis=1)
    reps = HEAD_V_DIM // HEAD_QK_DIM
    return jnp.tile(cos64, (1, reps)), jnp.tile(sin64, (1, reps))


def _inproj_kernel(x_ref, sh_ref, sc_ref, w_ref, cos_ref, sin_ref,
                   u_ref, g_ref, q_ref, k_ref, v_ref, *, lru_w, n_heads, rope):
    h = _layer_norm(x_ref[0]) * (1.0 + sc_ref[0]) + sh_ref[0]
    p = _dot(h.astype(BF16), w_ref[...])
    u_ref[0] = p[:, :lru_w]
    g_ref[0] = p[:, lru_w:2 * lru_w].astype(BF16)
    qk_w = n_heads * HEAD_V_DIM
    base_q = 2 * lru_w
    base_k = base_q + qk_w
    base_v = base_k + qk_w
    if rope:
        cos = cos_ref[...]
        sin = sin_ref[...]
        lane = lax.broadcasted_iota(jnp.int32, cos.shape, 1)
        first = (lane % 32) < 16
    q_scale = HEAD_QK_DIM ** -0.5 * math.log2(math.e)
    for hd in range(n_heads):
        lo, hi = hd * HEAD_V_DIM, (hd + 1) * HEAD_V_DIM
        for base, ref, scale in ((base_q, q_ref, q_scale), (base_k, k_ref, None)):
            t = p[:, base + lo:base + hi]
            if rope:
                partner = jnp.where(first, pltpu.roll(t, LANES - 16, 1), pltpu.roll(t, 16, 1))
                t = t * cos + partner * sin
            if scale is not None:
                t = t * scale
            ref[0, hd] = t.astype(BF16)
        v_ref[0, hd] = p[:, base_v + lo:base_v + hi].astype(BF16)


def _inproj(x, shift, scale, w_in, cos, sin, *, lru_w, n_heads, rope):
    bsz, seq, d = x.shape
    in_w = w_in.shape[1]
    tm = min(seq, 512)
    kern = functools.partial(_inproj_kernel, lru_w=lru_w, n_heads=n_heads, rope=rope)
    head_shape = jax.ShapeDtypeStruct((bsz, n_heads, seq, HEAD_V_DIM), BF16)
    head_spec = pl.BlockSpec((1, n_heads, tm, HEAD_V_DIM), lambda b, i: (b, 0, i, 0))
    vec_spec = pl.BlockSpec((1, 1, d), lambda b, i: (b, 0, 0))
    return pl.pallas_call(
        kern,
        out_shape=(jax.ShapeDtypeStruct((bsz, seq, lru_w), F32),
                   jax.ShapeDtypeStruct((bsz, seq, lru_w), BF16),
                   head_shape, head_shape, head_shape),
        grid=(bsz, seq // tm),
        in_specs=[pl.BlockSpec((1, tm, d), lambda b, i: (b, i, 0)),
                  vec_spec, vec_spec,
                  pl.BlockSpec((d, in_w), lambda b, i: (0, 0)),
                  pl.BlockSpec((tm, HEAD_V_DIM), lambda b, i: (i, 0)),
                  pl.BlockSpec((tm, HEAD_V_DIM), lambda b, i: (i, 0))],
        out_specs=(pl.BlockSpec((1, tm, lru_w), lambda b, i: (b, i, 0)),
                   pl.BlockSpec((1, tm, lru_w), lambda b, i: (b, i, 0)),
                   head_spec, head_spec, head_spec),
        compiler_params=_cparams("parallel", "arbitrary"),
        name="inproj_rope" if rope else "inproj_ctx",
    )(x, shift, scale, w_in, cos, sin)


def _scan_rows(a, b, reverse):
    row = lax.broadcasted_iota(jnp.int32, a.shape, 0)
    for s in (1, 2, 4):
        shift = SUBLANES - s if reverse else s
        a_sh = pltpu.roll(a, shift, 0)
        b_sh = pltpu.roll(b, shift, 0)
        valid = (row < SUBLANES - s) if reverse else (row >= s)
        b = jnp.where(valid, a * b_sh + b, b)
        a = jnp.where(valid, a * a_sh, a)
    return a, b


def _lru_kernel(*refs, reverse, combine, n_chunks, chunk):
    if combine:
        (u_ref, up_ref, un_ref, cw_ref, cb_ref, wg_ref, bg_ref, lam_ref, h0_ref, hb_ref, g_ref,
         out_ref, hn_ref, carry_ref, a_scr, b_scr) = refs
    else:
        (u_ref, up_ref, un_ref, cw_ref, cb_ref, wg_ref, bg_ref, lam_ref, h0_ref,
         out_ref, hn_ref, carry_ref, a_scr, b_scr) = refs
    c = pl.program_id(1)
    cc = (n_chunks - 1 - c) if reverse else c
    width = u_ref.shape[-1]

    @pl.when(c == 0)
    def _():
        carry_ref[...] = h0_ref[0]

    u = u_ref[0]
    prev = jnp.where(cc > 0, up_ref[0], 0.0)
    nxt = jnp.where(cc < n_chunks - 1, un_ref[0], 0.0)
    row = lax.broadcasted_iota(jnp.int32, u.shape, 0)
    u_m1 = jnp.where(row == 0, prev[7:8], pltpu.roll(u, 1, 0))
    u_m2 = jnp.where(row == 0, prev[6:7], jnp.where(row == 1, prev[7:8], pltpu.roll(u, 2, 0)))
    u_p1 = jnp.where(row == chunk - 1, nxt[0:1], pltpu.roll(u, chunk - 1, 0))
    cw = cw_ref[...]
    xc = cb_ref[...] + cw[0:1] * u_m2 + cw[1:2] * u_m1 + cw[2:3] * u + cw[3:4] * u_p1

    z = _dot(xc.astype(BF16), wg_ref[...]) + bg_ref[...]
    r = _sigmoid(z[:, :width])
    i = _sigmoid(z[:, width:])
    nlam = -lam_ref[...]
    softplus = jnp.maximum(nlam, 0.0) + jnp.log(1.0 + jnp.exp(-jnp.abs(nlam)))
    a = jnp.exp(-LRU_C * r * softplus)
    a_scr[...] = a
    gap = 1.0 - a * a
    b_scr[...] = jnp.where(gap > 0.0, gap * lax.rsqrt(gap), 0.0) * (i * xc)

    n_groups = chunk // SUBLANES

    def body(j, carry):
        jj = (n_groups - 1 - j) if reverse else j
        r0 = pl.multiple_of(jj * SUBLANES, SUBLANES)
        a_cum, h_loc = _scan_rows(a_scr[pl.ds(r0, SUBLANES), :], b_scr[pl.ds(r0, SUBLANES), :], reverse)
        h = a_cum * carry + h_loc
        b_scr[pl.ds(r0, SUBLANES), :] = h
        return h[0:1] if reverse else h[SUBLANES - 1:SUBLANES]

    carry = lax.fori_loop(0, n_groups, body, carry_ref[...])
    carry_ref[...] = carry
    hn_ref[0] = carry
    h_all = b_scr[...]
    if combine:
        out_ref[0] = (jax.nn.gelu(g_ref[0].astype(F32)) * (h_all + hb_ref[0])).astype(out_ref.dtype)
    else:
        out_ref[0] = h_all


def _lru(u, conv_w, conv_b, w_gate, b_gate, lam, h0, *, reverse, other=None, gate=None):
    bsz, seq, width = u.shape
    chunk = min(seq, 512)
    n_chunks = seq // chunk
    halo_per_chunk = chunk // SUBLANES
    n_halo = seq // SUBLANES
    combine = other is not None

    def pos(c):
        return (n_chunks - 1 - c) if reverse else c

    tile = lambda b, c: (b, pos(c), 0)
    prev_halo = lambda b, c: (b, jnp.maximum(pos(c) * halo_per_chunk - 1, 0), 0)
    next_halo = lambda b, c: (b, jnp.minimum((pos(c) + 1) * halo_per_chunk, n_halo - 1), 0)
    const2 = lambda b, c: (0, 0)
    state = lambda b, c: (b, 0, 0)
    in_specs = [pl.BlockSpec((1, chunk, width), tile),
                pl.BlockSpec((1, SUBLANES, width), prev_halo),
                pl.BlockSpec((1, SUBLANES, width), next_halo),
                pl.BlockSpec((CONV_W, width), const2),
                pl.BlockSpec((1, width), const2),
                pl.BlockSpec((width, 2 * width), const2),
                pl.BlockSpec((1, 2 * width), const2),
                pl.BlockSpec((1, width), const2),
                pl.BlockSpec((1, 1, width), state)]
    args = [u, u, u, conv_w, conv_b, w_gate, b_gate, lam, h0]
    if combine:
        in_specs += [pl.BlockSpec((1, chunk, width), tile), pl.BlockSpec((1, chunk, width), tile)]
        args += [other, gate]
    kern = functools.partial(_lru_kernel, reverse=reverse, combine=combine,
                             n_chunks=n_chunks, chunk=chunk)
    return pl.pallas_call(
        kern,
        out_shape=(jax.ShapeDtypeStruct((bsz, seq, width), BF16 if combine else F32),
                   jax.ShapeDtypeStruct((bsz, 1, width), F32)),
        grid=(bsz, n_chunks),
        in_specs=in_specs,
        out_specs=(pl.BlockSpec((1, chunk, width), tile), pl.BlockSpec((1, 1, width), state)),
        scratch_shapes=[pltpu.VMEM((1, width), F32),
                        pltpu.VMEM((chunk, width), F32),
                        pltpu.VMEM((chunk, width), F32)],
        compiler_params=_cparams("parallel", "arbitrary"),
        name=("lru_bwd" if reverse else "lru_fwd") + ("_mix" if combine else ""),
    )(*args)


def _expand_block_diag(w):
    n, k, _ = w.shape
    eye = jnp.eye(n, dtype=w.dtype)
    return (eye[:, None, :, None] * w[:, :, None, :]).reshape(n * k, n * k)


def _attn_kernel(dl_ref, q_ref, k_ref, v_ref, gn_ref, o_ref,
                 q2_scr, m_scr, acc_scr, s_scr, p_scr, a_scr,
                 *, tk, rb, unroll, lam_init):
    tq = q_ref.shape[2]
    dv = v_ref.shape[3]
    q = q_ref[0, 0]
    lane = lax.broadcasted_iota(jnp.int32, q.shape, 1)
    zero = jnp.zeros_like(q)
    q2_scr[:tq, :] = jnp.where(lane < HEAD_QK_DIM, q, zero)
    q2_scr[tq:, :] = jnp.where(lane >= HEAD_QK_DIM, q, zero)
    m_scr[...] = jnp.full(m_scr.shape, -jnp.inf, F32)
    acc_scr[...] = jnp.zeros(acc_scr.shape, F32)

    n_keys = k_ref.shape[2]
    n_chunks = pl.cdiv(n_keys, tk)

    def chunk_len(j):
        return min(tk, n_keys - j * tk) if isinstance(j, int) else tk

    def kv_chunk(ref, j):
        start = j * tk if isinstance(j, int) else pl.multiple_of(j * tk, tk)
        return ref[0, 0, pl.ds(start, chunk_len(j)), :]

    def put_scores(buf, j):
        s_scr[buf, :, pl.ds(0, chunk_len(j))] = lax.dot_general(
            q2_scr[...], kv_chunk(k_ref, j), (((1,), (1,)), ((), ())), preferred_element_type=F32)

    def softmax_rows(buf, n, part, parts):
        n_blocks = 2 * tq // rb
        for r in range(part * n_blocks // parts, (part + 1) * n_blocks // parts):
            rows = pl.ds(r * rb, rb)
            s = s_scr[buf, rows, pl.ds(0, n)]
            m_prev = m_scr[rows, :]
            m_new = jnp.maximum(m_prev, jnp.max(s, axis=-1, keepdims=True))
            p_scr[buf, rows, pl.ds(0, n)] = jnp.exp2(s - m_new).astype(BF16)
            a_scr[buf, rows, :] = jnp.exp2(m_prev - m_new)
            m_scr[rows, :] = m_new

    def accumulate(buf, j):
        n = chunk_len(j)
        ones = jnp.where(lax.broadcasted_iota(jnp.int32, (n, dv), 1) == 0, 1.0, 0.0).astype(BF16)
        v_ext = jnp.concatenate([kv_chunk(v_ref, j), ones], axis=1)
        acc_scr[...] = a_scr[buf] * acc_scr[...] + _dot(p_scr[buf, :, pl.ds(0, n)], v_ext)

    put_scores(0, 0)
    p_scr[1] = jnp.zeros(p_scr.shape[1:], BF16)
    a_scr[1] = jnp.ones(a_scr.shape[1:], F32)

    def step(j, cur, with_scores):
        prev = max(j - 1, 0) if isinstance(j, int) else jnp.maximum(j - 1, 0)
        n = chunk_len(j)
        softmax_rows(cur, n, 0, 4)
        if with_scores:
            put_scores(1 - cur, j + 1)
        softmax_rows(cur, n, 1, 4)
        softmax_rows(cur, n, 2, 4)
        accumulate(1 - cur, prev)
        softmax_rows(cur, n, 3, 4)

    def body(jj, carry):
        for sub in range(unroll):
            step(unroll * jj + sub, sub % 2, True)
        return carry

    n_loops = max(n_keys // tk - 1, 0) // unroll
    lax.fori_loop(0, n_loops, body, 0)
    for j in range(unroll * n_loops, n_chunks):
        step(j, j % 2, j + 1 < n_chunks)
    accumulate((n_chunks - 1) % 2, n_chunks - 1)

    lp = dl_ref[...]
    lam = (jnp.exp(jnp.sum(lp[0:1] * lp[1:2], axis=-1, keepdims=True))
           - jnp.exp(jnp.sum(lp[2:3] * lp[3:4], axis=-1, keepdims=True)) + lam_init)
    o_all = acc_scr[:, :dv] / acc_scr[:, dv:dv + 1]
    o = o_all[:tq] - lam * o_all[tq:]
    y = o * lax.rsqrt(jnp.mean(o * o, axis=-1, keepdims=True) + LN_EPS) * (1.0 - lam_init)
    o_ref[0] = (y * gn_ref[0]).astype(o_ref.dtype)


def _attention(diff_lambda, q, k, v, k_c, v_c, norm_g, *, lam_init):
    bsz, n_heads, seq, dv = q.shape
    n_valid = seq + k_c.shape[2]
    tq = min(seq, 512)
    tk = min(seq // 2, 512)
    rb = min(2 * tq, 32)
    assert n_valid % LANES == 0
    k_all = jnp.concatenate([k, k_c], axis=2)
    v_all = jnp.concatenate([v, v_c], axis=2)
    unroll = 2
    kern = functools.partial(_attn_kernel, tk=tk, rb=rb, unroll=unroll, lam_init=lam_init)
    kv_spec = pl.BlockSpec((1, 1, n_valid, dv), lambda b, h, i: (b, h, 0, 0))
    return pl.pallas_call(
        kern,
        out_shape=jax.ShapeDtypeStruct((bsz, seq, n_heads * dv), BF16),
        grid=(bsz, n_heads, seq // tq),
        in_specs=[pl.BlockSpec(diff_lambda.shape, lambda b, h, i: (0, 0)),
                  pl.BlockSpec((1, 1, tq, dv), lambda b, h, i: (b, h, i, 0)),
                  kv_spec, kv_spec,
                  pl.BlockSpec((1, 1, dv), lambda b, h, i: (h, 0, 0))],
        out_specs=pl.BlockSpec((1, tq, dv), lambda b, h, i: (b, i, h)),
        scratch_shapes=[pltpu.VMEM((2 * tq, dv), BF16),
                        pltpu.VMEM((2 * tq, 1), F32),
                        pltpu.VMEM((2 * tq, 2 * dv), F32),
                        pltpu.VMEM((2, 2 * tq, tk), F32),
                        pltpu.VMEM((2, 2 * tq, tk), BF16),
                        pltpu.VMEM((2, 2 * tq, 1), F32)],
        compiler_params=_cparams("parallel", "parallel", "arbitrary"),
        name="diff_attn",
    )(diff_lambda, q, k_all, v_all, norm_g)


def _mid_kernel(x_ref, yl_ref, ya_ref, wo_ref, g1_ref, sh2_ref, sc2_ref, pg_ref, pb_ref,
                wrh_ref, wrl_ref, br_ref, xmid_ref, h2_ref, route_ref, counts_ref, cnt_scr,
                *, alpha, n_groups, per_group):
    @pl.when((pl.program_id(0) == 0) & (pl.program_id(1) == 0))
    def _():
        cnt_scr[...] = jnp.zeros(cnt_scr.shape, F32)

    lru_w = yl_ref.shape[-1]
    mix = _dot(yl_ref[0], wo_ref[:lru_w, :]) + _dot(ya_ref[0], wo_ref[lru_w:, :])
    x_mid = _layer_norm(alpha * x_ref[0] + g1_ref[0] * mix) * pg_ref[...] + pb_ref[...]
    xmid_ref[0] = x_mid
    h2 = _layer_norm(x_mid) * (1.0 + sc2_ref[0]) + sh2_ref[0]
    h2_ref[0] = h2

    h_hi, h_lo = _split_bf16(h2)
    logits = _dot_split(h_hi, h_lo, wrh_ref[...], wrl_ref[...]) + br_ref[...]
    lane = lax.broadcasted_iota(jnp.int32, logits.shape, 1)

    def first_argmax(vals, vmax):
        return jnp.min(jnp.where(vals == vmax, lane, LANES), axis=-1, keepdims=True)

    gl = jnp.where(lane < n_groups, logits, NEG_BIG)
    g_max = jnp.max(gl, axis=-1, keepdims=True)
    p_g = 1.0 / jnp.sum(jnp.exp(gl - g_max), axis=-1, keepdims=True)
    g_sel = first_argmax(gl, g_max)
    e_lo = n_groups + g_sel * per_group
    el = jnp.where((lane >= e_lo) & (lane < e_lo + per_group), logits, NEG_BIG)
    m1 = jnp.max(el, axis=-1, keepdims=True)
    i1 = first_argmax(el, m1)
    el2 = jnp.where(lane == i1, NEG_BIG, el)
    m2 = jnp.max(el2, axis=-1, keepdims=True)
    i2 = first_argmax(el2, m2)
    e2 = jnp.exp(m2 - m1)
    w1 = 1.0 / (1.0 + e2)
    w2 = e2 * w1

    ex1 = i1 - n_groups
    ex2 = i2 - n_groups
    oh1 = jnp.where(lane == ex1, 1.0, 0.0)
    oh2 = jnp.where(lane == ex2, 1.0, 0.0)
    both = oh1 + oh2
    tm = both.shape[0]
    tri = (lax.broadcasted_iota(jnp.int32, (tm, tm), 0)
           > lax.broadcasted_iota(jnp.int32, (tm, tm), 1)).astype(BF16)
    before = _dot(tri, both.astype(BF16)) + cnt_scr[...]
    rank1 = jnp.sum(oh1 * before, axis=-1, keepdims=True)
    rank2 = jnp.sum(oh2 * before, axis=-1, keepdims=True)
    cnt_scr[...] += jnp.sum(both, axis=0, keepdims=True)
    counts_ref[...] = cnt_scr[...]

    fields = (ex1.astype(F32), ex2.astype(F32), p_g * w1, p_g * w2, rank1, rank2)
    slab = jnp.zeros(logits.shape, F32)
    for idx, val in enumerate(fields):
        slab = jnp.where(lane == idx, val, slab)
    route_ref[0] = slab


def _mid(x, y_lru, y_att, w_out, g1, sh2, sc2, post_g, post_b, wr_hi, wr_lo, b_r,
         *, alpha, n_groups, per_group):
    bsz, seq, d = x.shape
    lru_w = y_lru.shape[-1]
    att_w = y_att.shape[-1]
    tm = min(seq, 512)
    kern = functools.partial(_mid_kernel, alpha=alpha, n_groups=n_groups, per_group=per_group)
    tok = lambda w: pl.BlockSpec((1, tm, w), lambda b, i: (b, i, 0))
    vec = pl.BlockSpec((1, 1, d), lambda b, i: (b, 0, 0))
    const = lambda r, c: pl.BlockSpec((r, c), lambda b, i: (0, 0))
    return pl.pallas_call(
        kern,
        out_shape=(jax.ShapeDtypeStruct((bsz, seq, d), F32),
                   jax.ShapeDtypeStruct((bsz, seq, d), F32),
                   jax.ShapeDtypeStruct((bsz, seq, LANES), F32),
                   jax.ShapeDtypeStruct((1, LANES), F32)),
        grid=(bsz, seq // tm),
        in_specs=[tok(d), tok(lru_w), tok(att_w), const(lru_w + att_w, d), vec, vec, vec,
                  const(1, d), const(1, d), const(d, LANES), const(d, LANES), const(1, LANES)],
        out_specs=(tok(d), tok(d), tok(LANES), const(1, LANES)),
        scratch_shapes=[pltpu.VMEM((1, LANES), F32)],
        compiler_params=_cparams("arbitrary", "arbitrary"),
        name="mid_router",
    )(x, y_lru, y_att, w_out, g1, sh2, sc2, post_g, post_b, wr_hi, wr_lo, b_r)


def _row_copy(src, src_row, dst, dst_row, sem):
    return pltpu.make_async_copy(src.at[pl.ds(src_row, 1)], dst.at[pl.ds(dst_row, 1)], sem)


def _drain_rows(src, dst, sem, n):
    pltpu.make_async_copy(src.at[pl.ds(0, n)], dst.at[pl.ds(0, n)], sem).wait()


def _dispatch_kernel(slot_ref, h_ref, init_hbm, out_hbm, sem, *, tile):
    del init_hbm

    def issue(t, carry):
        for k in range(2):
            _row_copy(h_ref, t, out_hbm, slot_ref[0, 0, 2 * t + k], sem).start()
        return carry

    lax.fori_loop(0, tile, issue, 0, unroll=ISSUE_UNROLL)
    for _ in range(2):
        _drain_rows(h_ref, out_hbm, sem, tile)


def _dispatch(h2, slots, n_rows, *, tile):
    n, d = h2.shape
    n_steps = n // tile
    return pl.pallas_call(
        functools.partial(_dispatch_kernel, tile=tile),
        out_shape=jax.ShapeDtypeStruct((n_rows, d), F32),
        grid=(n_steps,),
        in_specs=[pl.BlockSpec((1, 1, 2 * tile), lambda i: (i, 0, 0), memory_space=pltpu.SMEM),
                  pl.BlockSpec((tile, d), lambda i: (i, 0)),
                  pl.BlockSpec(memory_space=pl.ANY)],
        out_specs=pl.BlockSpec(memory_space=pl.ANY),
        scratch_shapes=[pltpu.SemaphoreType.DMA(())],
        input_output_aliases={2: 0},
        compiler_params=_cparams("arbitrary"),
        name="moe_dispatch",
    )(slots.reshape(n_steps, 1, 2 * tile), h2, jnp.zeros((n_rows, d), F32))


def _experts_kernel(te_ref, nu_ref, h_ref, w1_ref, w3_ref, w2_ref, y_ref):
    del te_ref
    used = pl.program_id(0) < nu_ref[0]

    @pl.when(used)
    def _():
        h = h_ref[...].astype(BF16)
        a = _dot(h, w1_ref[0].astype(BF16))
        act = a * _sigmoid(a) * _dot(h, w3_ref[0].astype(BF16))
        y_ref[...] = _dot(act.astype(BF16), w2_ref[0].astype(BF16))

    @pl.when(jnp.logical_not(used))
    def _():
        y_ref[...] = jnp.zeros(y_ref.shape, F32)


def _experts(hs, tile_expert, n_used, w1, w3, w2, *, tile):
    n_rows, d = hs.shape
    d_e = w1.shape[2]
    used = lambda i, te, nu: jnp.minimum(i, nu[0] - 1)
    row_spec = pl.BlockSpec((tile, d), lambda i, te, nu: (used(i, te, nu), 0))
    w_spec = lambda r, c: pl.BlockSpec((1, r, c), lambda i, te, nu: (te[used(i, te, nu)], 0, 0))
    return pl.pallas_call(
        _experts_kernel,
        out_shape=jax.ShapeDtypeStruct((n_rows, d), F32),
        grid_spec=pltpu.PrefetchScalarGridSpec(
            num_scalar_prefetch=2,
            grid=(n_rows // tile,),
            in_specs=[row_spec, w_spec(d, d_e), w_spec(d, d_e), w_spec(d_e, d)],
            out_specs=pl.BlockSpec((tile, d), lambda i, te, nu: (i, 0))),
        compiler_params=_cparams("arbitrary"),
        name="moe_experts",
    )(tile_expert, n_used, hs, w1, w3, w2)


def _combine_kernel(slot_ref, route_ref, y_hbm, xmid_ref, g2_ref, pg_ref, pb_ref, out_ref,
                    ya_scr, yb_scr, sem, *, alpha, tile):
    def issue(t, carry):
        _row_copy(y_hbm, slot_ref[0, 0, 2 * t], ya_scr, t, sem).start()
        _row_copy(y_hbm, slot_ref[0, 0, 2 * t + 1], yb_scr, t, sem).start()
        return carry

    lax.fori_loop(0, tile, issue, 0, unroll=ISSUE_UNROLL)
    _drain_rows(y_hbm, ya_scr, sem, tile)
    _drain_rows(y_hbm, yb_scr, sem, tile)
    route = route_ref[...]
    y = route[:, 2:3] * ya_scr[...] + route[:, 3:4] * yb_scr[...]
    z = alpha * xmid_ref[...] + g2_ref[0] * y
    out_ref[...] = _layer_norm(z) * pg_ref[...] + pb_ref[...]


def _combine(ys, slots, route, x_mid, g2, post_g, post_b, *, alpha, seq, tile):
    n, d = x_mid.shape
    n_steps = n // tile
    assert seq % tile == 0
    tok = lambda w: pl.BlockSpec((tile, w), lambda i: (i, 0))
    const = pl.BlockSpec((1, d), lambda i: (0, 0))
    return pl.pallas_call(
        functools.partial(_combine_kernel, alpha=alpha, tile=tile),
        out_shape=jax.ShapeDtypeStruct((n, d), F32),
        grid=(n_steps,),
        in_specs=[pl.BlockSpec((1, 1, 2 * tile), lambda i: (i, 0, 0), memory_space=pltpu.SMEM),
                  tok(LANES),
                  pl.BlockSpec(memory_space=pl.ANY),
                  tok(d),
                  pl.BlockSpec((1, 1, d), lambda i: (i * tile // seq, 0, 0)),
                  const, const],
        out_specs=tok(d),
        scratch_shapes=[pltpu.VMEM((tile, d), F32), pltpu.VMEM((tile, d), F32),
                        pltpu.SemaphoreType.DMA(())],
        compiler_params=_cparams("arbitrary"),
        name="moe_combine",
    )(slots.reshape(n_steps, 1, 2 * tile), route, ys, x_mid, g2, post_g, post_b)


def _moe(h2, route, counts, w1, w3, w2, x_mid, g2, post_g, post_b, *, alpha):
    bsz, seq, d = x_mid.shape
    n_experts = w1.shape[0]
    n = bsz * seq
    tile_e = 256
    tile_t = min(seq, 512)
    n_tiles = 2 * n // tile_e + n_experts
    route = route.reshape(n, LANES)
    expert = route[:, 0:2].astype(jnp.int32)
    rank = route[:, 4:6].astype(jnp.int32)
    cnt = counts[0, :n_experts].astype(jnp.int32)
    tiles = (cnt + tile_e - 1) // tile_e
    tile_end = jnp.cumsum(tiles)
    slots = ((tile_end - tiles) * tile_e)[expert] + rank
    tile_ids = jnp.arange(n_tiles, dtype=jnp.int32)
    tile_expert = jnp.minimum(jnp.sum((tile_end[None, :] <= tile_ids[:, None]).astype(jnp.int32), axis=1),
                              n_experts - 1)
    n_used = tile_end[-1:].astype(jnp.int32)
    hs = _dispatch(h2.reshape(n, d), slots, n_tiles * tile_e, tile=tile_t)
    ys = _experts(hs, tile_expert, n_used, w1, w3, w2, tile=tile_e)
    out = _combine(ys, slots, route, x_mid.reshape(n, d), g2, post_g, post_b,
                   alpha=alpha, seq=seq, tile=tile_t)
    return out.reshape(bsz, seq, d)


def _block(x, c, ctx, c_ctx, w_mod, b_mod, w_in, conv_w, conv_b, lru_wa, lru_ba, lru_wi, lru_bi,
           lru_lambda, diff_lambda, attn_norm_g, w_out, post_g, post_b, router_g_w, router_g_b,
           router_e_w, router_e_b, exp_w1, exp_w3, exp_w2, *, grid_w):
    depth = w_mod.shape[0]
    assert depth == 1, "single-layer block only"
    bsz, seq, d = x.shape
    lru_w = conv_w.shape[-1]
    n_heads, dv = attn_norm_g.shape[1:]
    assert dv == HEAD_V_DIM and conv_w.shape[1] == CONV_W
    n_groups, _, per_group = router_e_w.shape[1:]
    assert n_groups + n_groups * per_group <= LANES
    alpha = (2.0 * depth) ** 0.25
    lam_init = 0.8 - 0.6 * math.exp(0.0)

    cond = jnp.concatenate([c, c_ctx[None], jnp.zeros((SUBLANES - 1 - bsz % SUBLANES, d), F32)], axis=0)
    mods = _mods(cond, w_mod[0], b_mod[0]).reshape(cond.shape[0], 6, 1, d)
    sh1, sc1, g1, sh2, sc2, g2 = (mods[:bsz, j] for j in range(6))
    sh1c, sc1c = (jnp.broadcast_to(mods[bsz:bsz + 1, j], (bsz, 1, d)) for j in range(2))

    w_in_b = w_in[0].astype(BF16)
    cos, sin = _rope_tables(seq, grid_w)
    u, gt, q, k, v = _inproj(x, sh1, sc1, w_in_b, cos, sin, lru_w=lru_w, n_heads=n_heads, rope=True)
    n_ctx = ctx.shape[1]
    u_c, _, _, k_c, v_c = _inproj(ctx, sh1c, sc1c, w_in_b, cos[:n_ctx], sin[:n_ctx],
                                  lru_w=lru_w, n_heads=n_heads, rope=False)

    zero_state = jnp.zeros((bsz, 1, lru_w), F32)
    cb = conv_b[0].reshape(1, lru_w)

    def gate_params(direction):
        w_gate = jnp.concatenate([_expand_block_diag(lru_wa[0, direction]),
                                  _expand_block_diag(lru_wi[0, direction])], axis=1).astype(BF16)
        b_gate = jnp.concatenate([lru_ba[0, direction], lru_bi[0, direction]]).reshape(1, 2 * lru_w)
        return w_gate, b_gate, lru_lambda[0, direction].reshape(1, lru_w)

    fwd_p, bwd_p = gate_params(0), gate_params(1)
    _, seed_f = _lru(u_c, conv_w[0], cb, *fwd_p, zero_state, reverse=False)
    _, seed_b = _lru(u_c, conv_w[0], cb, *bwd_p, zero_state, reverse=True)
    h_bwd, _ = _lru(u, conv_w[0], cb, *bwd_p, seed_b, reverse=True)
    y_lru, _ = _lru(u, conv_w[0], cb, *fwd_p, seed_f, reverse=False, other=h_bwd, gate=gt)

    y_att = _attention(diff_lambda[0], q, k, v, k_c, v_c, attn_norm_g[0].reshape(n_heads, 1, dv),
                       lam_init=lam_init)

    w_r = jnp.concatenate([router_g_w[0], jnp.moveaxis(router_e_w[0], 0, 1).reshape(d, -1)], axis=1)
    b_r = jnp.concatenate([router_g_b[0], router_e_b[0].reshape(-1)])
    pad = LANES - w_r.shape[1]
    w_r = jnp.pad(w_r, ((0, 0), (0, pad)))
    b_r = jnp.pad(b_r, (0, pad)).reshape(1, LANES)
    wr_hi, wr_lo = _split_bf16(w_r)

    x_mid, h2, route, counts = _mid(x, y_lru, y_att, w_out[0].astype(BF16), g1, sh2, sc2,
                                    post_g[0, 0].reshape(1, d), post_b[0, 0].reshape(1, d),
                                    wr_hi, wr_lo, b_r, alpha=alpha, n_groups=n_groups,
                                    per_group=per_group)
    return _moe(h2, route, counts, exp_w1[0], exp_w3[0], exp_w2[0], x_mid, g2,
                post_g[0, 1].reshape(1, d),
                post_b[0, 1].reshape(1, d), alpha=alpha)


def kernel(x, c, ctx, c_ctx, w_mod, b_mod, w_in, conv_w, conv_b, lru_wa, lru_ba, lru_wi, lru_bi, lru_lambda, diff_lambda, attn_norm_g, w_out, post_g, post_b, router_g_w, router_g_b, router_e_w, router_e_b, exp_w1, exp_w3, exp_w2):
    return _block(x, c, ctx, c_ctx, w_mod, b_mod, w_in, conv_w, conv_b, lru_wa, lru_ba, lru_wi,
                  lru_bi, lru_lambda, diff_lambda, attn_norm_g, w_out, post_g, post_b, router_g_w,
                  router_g_b, router_e_w, router_e_b, exp_w1, exp_w3, exp_w2, grid_w=GRID_W)
```

```python
import functools
import math

import jax
import jax.numpy as jnp
from jax import lax
from jax.experimental import pallas as pl
from jax.experimental.pallas import tpu as pltpu

F32 = jnp.float32
BF16 = jnp.bfloat16

LN_EPS = 1e-5
LRU_C = 8.0
ROPE_BASE = 10000.0
GRID_W = 64
HEAD_V_DIM = 128
HEAD_QK_DIM = HEAD_V_DIM // 2
CONV_W = 4
SUBLANES = 8
LANES = 128
NEG_BIG = -1e30
VMEM_LIMIT_BYTES = 56 * 1024 * 1024
ROUTE_W = 8
ISSUE_UNROLL = 8


def _cparams(*sem):
    return pltpu.CompilerParams(dimension_semantics=sem, vmem_limit_bytes=VMEM_LIMIT_BYTES)


def _layer_norm(x):
    mu = jnp.mean(x, axis=-1, keepdims=True)
    xc = x - mu
    var = jnp.mean(xc * xc, axis=-1, keepdims=True)
    return xc * lax.rsqrt(var + LN_EPS)


def _sigmoid(x):
    return 0.5 * jnp.tanh(0.5 * x) + 0.5


def _split_bf16(x):
    hi = x.astype(BF16)
    lo = (x - hi.astype(F32)).astype(BF16)
    return hi, lo


def _dot(a, b):
    return jnp.dot(a, b, preferred_element_type=F32)


def _dot_split(a_hi, a_lo, b_hi, b_lo):
    return _dot(a_hi, b_hi) + _dot(a_hi, b_lo) + _dot(a_lo, b_hi)


def _mods_kernel(c_ref, w_ref, b_ref, o_ref):
    c = c_ref[...]
    s = c * jax.nn.sigmoid(c)
    s_hi, s_lo = _split_bf16(s)
    w_hi, w_lo = _split_bf16(w_ref[...])
    o_ref[...] = _dot_split(s_hi, s_lo, w_hi, w_lo) + b_ref[...]


def _mods(cond, w, b):
    rows, d = cond.shape
    n = w.shape[1]
    tn = min(n, 1024)
    return pl.pallas_call(
        _mods_kernel,
        out_shape=jax.ShapeDtypeStruct((rows, n), F32),
        grid=(n // tn,),
        in_specs=[pl.BlockSpec((rows, d), lambda j: (0, 0)),
                  pl.BlockSpec((d, tn), lambda j: (0, j)),
                  pl.BlockSpec((1, tn), lambda j: (0, j))],
        out_specs=pl.BlockSpec((rows, tn), lambda j: (0, j)),
        compiler_params=_cparams("arbitrary"),
        name="mods",
    )(cond, w, b.reshape(1, n))


def _rope_tables(n_tokens, grid_w):
    n_freq = HEAD_QK_DIM // 4
    pos = jnp.arange(n_tokens, dtype=jnp.int32)
    row = (pos // grid_w).astype(F32)
    col = (pos % grid_w).astype(F32)
    inv = ROPE_BASE ** (-jnp.arange(n_freq, dtype=F32) / n_freq)
    ar = row[:, None] * inv
    ac = col[:, None] * inv
    cos64 = jnp.concatenate([jnp.cos(ar), jnp.cos(ar), jnp.cos(ac), jnp.cos(ac)], axis=1)
    sin64 = jnp.concatenate([-jnp.sin(ar), jnp.sin(ar), -jnp.sin(ac), jnp.sin(ac)], axis=1)
    reps = HEAD_V_DIM // HEAD_QK_DIM
    return jnp.tile(cos64, (1, reps)), jnp.tile(sin64, (1, reps))


def _inproj_kernel(x_ref, sh_ref, sc_ref, w_ref, cos_ref, sin_ref,
                   u_ref, g_ref, q_ref, k_ref, v_ref, *, lru_w, n_heads, rope):
    h = _layer_norm(x_ref[0]) * (1.0 + sc_ref[0]) + sh_ref[0]
    p = _dot(h.astype(BF16), w_ref[...])
    u_ref[0] = p[:, :lru_w]
    g_ref[0] = p[:, lru_w:2 * lru_w].astype(BF16)
    qk_w = n_heads * HEAD_V_DIM
    base_q = 2 * lru_w
    base_k = base_q + qk_w
    base_v = base_k + qk_w
    if rope:
        cos = cos_ref[...]
        sin = sin_ref[...]
        lane = lax.broadcasted_iota(jnp.int32, cos.shape, 1)
        first = (lane % 32) < 16
    q_scale = HEAD_QK_DIM ** -0.5 * math.log2(math.e)
    for hd in range(n_heads):
        lo, hi = hd * HEAD_V_DIM, (hd + 1) * HEAD_V_DIM
        for base, ref, scale in ((base_q, q_ref, q_scale), (base_k, k_ref, None)):
            t = p[:, base + lo:base + hi]
            if rope:
                partner = jnp.where(first, pltpu.roll(t, LANES - 16, 1), pltpu.roll(t, 16, 1))
                t = t * cos + partner * sin
            if scale is not None:
                t = t * scale
            ref[0, hd] = t.astype(BF16)
        v_ref[0, hd] = p[:, base_v + lo:base_v + hi].astype(BF16)


def _inproj(x, shift, scale, w_in, cos, sin, *, lru_w, n_heads, rope):
    bsz, seq, d = x.shape
    in_w = w_in.shape[1]
    tm = min(seq, 512)
    kern = functools.partial(_inproj_kernel, lru_w=lru_w, n_heads=n_heads, rope=rope)
    head_shape = jax.ShapeDtypeStruct((bsz, n_heads, seq, HEAD_V_DIM), BF16)
    head_spec = pl.BlockSpec((1, n_heads, tm, HEAD_V_DIM), lambda b, i: (b, 0, i, 0))
    vec_spec = pl.BlockSpec((1, 1, d), lambda b, i: (b, 0, 0))
    return pl.pallas_call(
        kern,
        out_shape=(jax.ShapeDtypeStruct((bsz, seq, lru_w), F32),
                   jax.ShapeDtypeStruct((bsz, seq, lru_w), BF16),
                   head_shape, head_shape, head_shape),
        grid=(bsz, seq // tm),
        in_specs=[pl.BlockSpec((1, tm, d), lambda b, i: (b, i, 0)),
                  vec_spec, vec_spec,
                  pl.BlockSpec((d, in_w), lambda b, i: (0, 0)),
                  pl.BlockSpec((tm, HEAD_V_DIM), lambda b, i: (i, 0)),
                  pl.BlockSpec((tm, HEAD_V_DIM), lambda b, i: (i, 0))],
        out_specs=(pl.BlockSpec((1, tm, lru_w), lambda b, i: (b, i, 0)),
                   pl.BlockSpec((1, tm, lru_w), lambda b, i: (b, i, 0)),
                   head_spec, head_spec, head_spec),
        compiler_params=_cparams("parallel", "arbitrary"),
        name="inproj_rope" if rope else "inproj_ctx",
    )(x, shift, scale, w_in, cos, sin)


def _scan_rows(a, b, reverse):
    row = lax.broadcasted_iota(jnp.int32, a.shape, 0)
    for s in (1, 2, 4):
        shift = SUBLANES - s if reverse else s
        a_sh = pltpu.roll(a, shift, 0)
        b_sh = pltpu.roll(b, shift, 0)
        valid = (row < SUBLANES - s) if reverse else (row >= s)
        b = jnp.where(valid, a * b_sh + b, b)
        a = jnp.where(valid, a * a_sh, a)
    return a, b


def _lru_kernel(*refs, reverse, combine, n_chunks, chunk):
    if combine:
        (u_ref, up_ref, un_ref, cw_ref, cb_ref, wg_ref, bg_ref, lam_ref, h0_ref, hb_ref, g_ref,
         out_ref, hn_ref, carry_ref, a_scr, b_scr) = refs
    else:
        (u_ref, up_ref, un_ref, cw_ref, cb_ref, wg_ref, bg_ref, lam_ref, h0_ref,
         out_ref, hn_ref, carry_ref, a_scr, b_scr) = refs
    c = pl.program_id(1)
    cc = (n_chunks - 1 - c) if reverse else c
    width = u_ref.shape[-1]

    @pl.when(c == 0)
    def _():
        carry_ref[...] = h0_ref[0]

    u = u_ref[0]
    prev = jnp.where(cc > 0, up_ref[0], 0.0)
    nxt = jnp.where(cc < n_chunks - 1, un_ref[0], 0.0)
    row = lax.broadcasted_iota(jnp.int32, u.shape, 0)
    u_m1 = jnp.where(row == 0, prev[7:8], pltpu.roll(u, 1, 0))
    u_m2 = jnp.where(row == 0, prev[6:7], jnp.where(row == 1, prev[7:8], pltpu.roll(u, 2, 0)))
    u_p1 = jnp.where(row == chunk - 1, nxt[0:1], pltpu.roll(u, chunk - 1, 0))
    cw = cw_ref[...]
    xc = cb_ref[...] + cw[0:1] * u_m2 + cw[1:2] * u_m1 + cw[2:3] * u + cw[3:4] * u_p1

    z = _dot(xc.astype(BF16), wg_ref[...]) + bg_ref[...]
    r = _sigmoid(z[:, :width])
    i = _sigmoid(z[:, width:])
    nlam = -lam_ref[...]
    softplus = jnp.maximum(nlam, 0.0) + jnp.log(1.0 + jnp.exp(-jnp.abs(nlam)))
    a = jnp.exp(-LRU_C * r * softplus)
    a_scr[...] = a
    gap = 1.0 - a * a
    b_scr[...] = jnp.where(gap > 0.0, gap * lax.rsqrt(gap), 0.0) * (i * xc)

    n_groups = chunk // SUBLANES

    def body(j, carry):
        jj = (n_groups - 1 - j) if reverse else j
        r0 = pl.multiple_of(jj * SUBLANES, SUBLANES)
        a_cum, h_loc = _scan_rows(a_scr[pl.ds(r0, SUBLANES), :], b_scr[pl.ds(r0, SUBLANES), :], reverse)
        h = a_cum * carry + h_loc
        b_scr[pl.ds(r0, SUBLANES), :] = h
        return h[0:1] if reverse else h[SUBLANES - 1:SUBLANES]

    carry = lax.fori_loop(0, n_groups, body, carry_ref[...])
    carry_ref[...] = carry
    hn_ref[0] = carry
    h_all = b_scr[...]
    if combine:
        out_ref[0] = (jax.nn.gelu(g_ref[0].astype(F32)) * (h_all + hb_ref[0])).astype(out_ref.dtype)
    else:
        out_ref[0] = h_all


def _lru(u, conv_w, conv_b, w_gate, b_gate, lam, h0, *, reverse, other=None, gate=None):
    bsz, seq, width = u.shape
    chunk = min(seq, 512)
    n_chunks = seq // chunk
    halo_per_chunk = chunk // SUBLANES
    n_halo = seq // SUBLANES
    combine = other is not None

    def pos(c):
        return (n_chunks - 1 - c) if reverse else c

    tile = lambda b, c: (b, pos(c), 0)
    prev_halo = lambda b, c: (b, jnp.maximum(pos(c) * halo_per_chunk - 1, 0), 0)
    next_halo = lambda b, c: (b, jnp.minimum((pos(c) + 1) * halo_per_chunk, n_halo - 1), 0)
    const2 = lambda b, c: (0, 0)
    state = lambda b, c: (b, 0, 0)
    in_specs = [pl.BlockSpec((1, chunk, width), tile),
                pl.BlockSpec((1, SUBLANES, width), prev_halo),
                pl.BlockSpec((1, SUBLANES, width), next_halo),
                pl.BlockSpec((CONV_W, width), const2),
                pl.BlockSpec((1, width), const2),
                pl.BlockSpec((width, 2 * width), const2),
                pl.BlockSpec((1, 2 * width), const2),
                pl.BlockSpec((1, width), const2),
                pl.BlockSpec((1, 1, width), state)]
    args = [u, u, u, conv_w, conv_b, w_gate, b_gate, lam, h0]
    if combine:
        in_specs += [pl.BlockSpec((1, chunk, width), tile), pl.BlockSpec((1, chunk, width), tile)]
        args += [other, gate]
    kern = functools.partial(_lru_kernel, reverse=reverse, combine=combine,
                             n_chunks=n_chunks, chunk=chunk)
    return pl.pallas_call(
        kern,
        out_shape=(jax.ShapeDtypeStruct((bsz, seq, width), BF16 if combine else F32),
                   jax.ShapeDtypeStruct((bsz, 1, width), F32)),
        grid=(bsz, n_chunks),
        in_specs=in_specs,
        out_specs=(pl.BlockSpec((1, chunk, width), tile), pl.BlockSpec((1, 1, width), state)),
        scratch_shapes=[pltpu.VMEM((1, width), F32),
                        pltpu.VMEM((chunk, width), F32),
                        pltpu.VMEM((chunk, width), F32)],
        compiler_params=_cparams("parallel", "arbitrary"),
        name=("lru_bwd" if reverse else "lru_fwd") + ("_mix" if combine else ""),
    )(*args)


def _expand_block_diag(w):
    n, k, _ = w.shape
    eye = jnp.eye(n, dtype=w.dtype)
    return (eye[:, None, :, None] * w[:, :, None, :]).reshape(n * k, n * k)


def _attn_kernel(dl_ref, q_ref, k_ref, v_ref, kc_ref, vc_ref, gn_ref, o_ref,
                 q2_scr, m_scr, acc_scr, s_scr, p_scr, a_scr,
                 *, tk, rb, unroll, lam_init):
    tq = q_ref.shape[2]
    dv = v_ref.shape[3]
    q = q_ref[0, 0]
    lane = lax.broadcasted_iota(jnp.int32, q.shape, 1)
    zero = jnp.zeros_like(q)
    q2_scr[:tq, :] = jnp.where(lane < HEAD_QK_DIM, q, zero)
    q2_scr[tq:, :] = jnp.where(lane >= HEAD_QK_DIM, q, zero)
    m_scr[...] = jnp.full(m_scr.shape, -jnp.inf, F32)
    acc_scr[...] = jnp.zeros(acc_scr.shape, F32)

    n_latent = k_ref.shape[2] // tk
    n_keys = k_ref.shape[2] + kc_ref.shape[2]
    n_chunks = n_latent + 1

    def chunk_len(j):
        return min(tk, n_keys - j * tk) if isinstance(j, int) else tk

    def kv_chunk(refs, j):
        latent, context = refs
        if isinstance(j, int):
            return context[0, 0] if j == n_latent else latent[0, 0, pl.ds(j * tk, tk), :]
        return latent[0, 0, pl.ds(pl.multiple_of(j * tk, tk), tk), :]

    keys = (k_ref, kc_ref)
    values = (v_ref, vc_ref)

    def put_scores(buf, j):
        s_scr[buf, :, pl.ds(0, chunk_len(j))] = lax.dot_general(
            q2_scr[...], kv_chunk(keys, j), (((1,), (1,)), ((), ())), preferred_element_type=F32)

    def softmax_rows(buf, n, part, parts):
        n_blocks = 2 * tq // rb
        for r in range(part * n_blocks // parts, (part + 1) * n_blocks // parts):
            rows = pl.ds(r * rb, rb)
            s = s_scr[buf, rows, pl.ds(0, n)]
            m_prev = m_scr[rows, :]
            m_new = jnp.maximum(m_prev, jnp.max(s, axis=-1, keepdims=True))
            p_scr[buf, rows, pl.ds(0, n)] = jnp.exp2(s - m_new).astype(BF16)
            a_scr[buf, rows, :] = jnp.exp2(m_prev - m_new)
            m_scr[rows, :] = m_new

    def accumulate(buf, j):
        n = chunk_len(j)
        ones = jnp.where(lax.broadcasted_iota(jnp.int32, (n, dv), 1) == 0, 1.0, 0.0).astype(BF16)
        v_ext = jnp.concatenate([kv_chunk(values, j), ones], axis=1)
        acc_scr[...] = a_scr[buf] * acc_scr[...] + _dot(p_scr[buf, :, pl.ds(0, n)], v_ext)

    put_scores(0, 0)
    p_scr[1] = jnp.zeros(p_scr.shape[1:], BF16)
    a_scr[1] = jnp.ones(a_scr.shape[1:], F32)

    def step(j, cur, with_scores):
        prev = max(j - 1, 0) if isinstance(j, int) else jnp.maximum(j - 1, 0)
        n = chunk_len(j)
        softmax_rows(cur, n, 0, 4)
        if with_scores:
            put_scores(1 - cur, j + 1)
        softmax_rows(cur, n, 1, 4)
        softmax_rows(cur, n, 2, 4)
        accumulate(1 - cur, prev)
        softmax_rows(cur, n, 3, 4)

    def body(jj, carry):
        for sub in range(unroll):
            step(unroll * jj + sub, sub % 2, True)
        return carry

    n_loops = (n_latent - 1) // unroll
    lax.fori_loop(0, n_loops, body, 0)
    for j in range(unroll * n_loops, n_chunks):
        step(j, j % 2, j + 1 < n_chunks)
    accumulate((n_chunks - 1) % 2, n_chunks - 1)

    lp = dl_ref[...]
    lam = (jnp.exp(jnp.sum(lp[0:1] * lp[1:2], axis=-1, keepdims=True))
           - jnp.exp(jnp.sum(lp[2:3] * lp[3:4], axis=-1, keepdims=True)) + lam_init)
    o_all = acc_scr[:, :dv] / acc_scr[:, dv:dv + 1]
    o = o_all[:tq] - lam * o_all[tq:]
    y = o * lax.rsqrt(jnp.mean(o * o, axis=-1, keepdims=True) + LN_EPS) * (1.0 - lam_init)
    o_ref[0] = (y * gn_ref[0]).astype(o_ref.dtype)


def _attention(diff_lambda, q, k, v, k_c, v_c, norm_g, *, lam_init):
    bsz, n_heads, seq, dv = q.shape
    n_ctx = k_c.shape[2]
    tq = min(seq, 512)
    tk = min(seq // 2, 512)
    rb = min(2 * tq, 32)
    assert seq % tk == 0 and n_ctx <= tk and n_ctx % LANES == 0
    unroll = 2
    kern = functools.partial(_attn_kernel, tk=tk, rb=rb, unroll=unroll, lam_init=lam_init)
    kv_spec = pl.BlockSpec((1, 1, seq, dv), lambda b, h, i: (b, h, 0, 0))
    ctx_spec = pl.BlockSpec((1, 1, n_ctx, dv), lambda b, h, i: (b, h, 0, 0))
    return pl.pallas_call(
        kern,
        out_shape=jax.ShapeDtypeStruct((bsz, seq, n_heads * dv), BF16),
        grid=(bsz, n_heads, seq // tq),
        in_specs=[pl.BlockSpec(diff_lambda.shape, lambda b, h, i: (0, 0)),
                  pl.BlockSpec((1, 1, tq, dv), lambda b, h, i: (b, h, i, 0)),
                  kv_spec, kv_spec, ctx_spec, ctx_spec,
                  pl.BlockSpec((1, 1, dv), lambda b, h, i: (h, 0, 0))],
        out_specs=pl.BlockSpec((1, tq, dv), lambda b, h, i: (b, i, h)),
        scratch_shapes=[pltpu.VMEM((2 * tq, dv), BF16),
                        pltpu.VMEM((2 * tq, 1), F32),
                        pltpu.VMEM((2 * tq, 2 * dv), F32),
                        pltpu.VMEM((2, 2 * tq, tk), F32),
                        pltpu.VMEM((2, 2 * tq, tk), BF16),
                        pltpu.VMEM((2, 2 * tq, 1), F32)],
        compiler_params=_cparams("parallel", "parallel", "arbitrary"),
        name="diff_attn",
    )(diff_lambda, q, k, v, k_c, v_c, norm_g)


def _mid_kernel(x_ref, yl_ref, ya_ref, wo_ref, g1_ref, sh2_ref, sc2_ref, pg_ref, pb_ref,
                wrh_ref, wrl_ref, br_ref, xmid_ref, h2_ref, route_ref, counts_ref, cnt_scr,
                *, alpha, n_groups, per_group):
    @pl.when((pl.program_id(0) == 0) & (pl.program_id(1) == 0))
    def _():
        cnt_scr[...] = jnp.zeros(cnt_scr.shape, F32)

    lru_w = yl_ref.shape[-1]
    mix = _dot(yl_ref[0], wo_ref[:lru_w, :]) + _dot(ya_ref[0], wo_ref[lru_w:, :])
    x_mid = _layer_norm(alpha * x_ref[0] + g1_ref[0] * mix) * pg_ref[...] + pb_ref[...]
    xmid_ref[0] = x_mid
    h2 = _layer_norm(x_mid) * (1.0 + sc2_ref[0]) + sh2_ref[0]
    h2_ref[0] = h2

    h_hi, h_lo = _split_bf16(h2)
    logits = _dot_split(h_hi, h_lo, wrh_ref[...], wrl_ref[...]) + br_ref[...]
    lane = lax.broadcasted_iota(jnp.int32, logits.shape, 1)

    def first_argmax(vals, vmax):
        return jnp.min(jnp.where(vals == vmax, lane, LANES), axis=-1, keepdims=True)

    gl = jnp.where(lane < n_groups, logits, NEG_BIG)
    g_max = jnp.max(gl, axis=-1, keepdims=True)
    p_g = 1.0 / jnp.sum(jnp.exp(gl - g_max), axis=-1, keepdims=True)
    g_sel = first_argmax(gl, g_max)
    e_lo = n_groups + g_sel * per_group
    el = jnp.where((lane >= e_lo) & (lane < e_lo + per_group), logits, NEG_BIG)
    m1 = jnp.max(el, axis=-1, keepdims=True)
    i1 = first_argmax(el, m1)
    el2 = jnp.where(lane == i1, NEG_BIG, el)
    m2 = jnp.max(el2, axis=-1, keepdims=True)
    i2 = first_argmax(el2, m2)
    e2 = jnp.exp(m2 - m1)
    w1 = 1.0 / (1.0 + e2)
    w2 = e2 * w1

    ex1 = i1 - n_groups
    ex2 = i2 - n_groups
    oh1 = jnp.where(lane == ex1, 1.0, 0.0)
    oh2 = jnp.where(lane == ex2, 1.0, 0.0)
    both = oh1 + oh2
    tm = both.shape[0]
    tri = (lax.broadcasted_iota(jnp.int32, (tm, tm), 0)
           > lax.broadcasted_iota(jnp.int32, (tm, tm), 1)).astype(BF16)
    before = _dot(tri, both.astype(BF16)) + cnt_scr[...]
    rank1 = jnp.sum(oh1 * before, axis=-1, keepdims=True)
    rank2 = jnp.sum(oh2 * before, axis=-1, keepdims=True)
    cnt_scr[...] += jnp.sum(both, axis=0, keepdims=True)
    counts_ref[...] = cnt_scr[...]

    fields = (ex1.astype(F32), ex2.astype(F32), p_g * w1, p_g * w2, rank1, rank2)
    slab = jnp.zeros(logits.shape, F32)
    for idx, val in enumerate(fields):
        slab = jnp.where(lane == idx, val, slab)
    route_ref[0] = slab[:, :ROUTE_W]


def _mid(x, y_lru, y_att, w_out, g1, sh2, sc2, post_g, post_b, wr_hi, wr_lo, b_r,
         *, alpha, n_groups, per_group):
    bsz, seq, d = x.shape
    lru_w = y_lru.shape[-1]
    att_w = y_att.shape[-1]
    tm = min(seq, 512)
    kern = functools.partial(_mid_kernel, alpha=alpha, n_groups=n_groups, per_group=per_group)
    tok = lambda w: pl.BlockSpec((1, tm, w), lambda b, i: (b, i, 0))
    vec = pl.BlockSpec((1, 1, d), lambda b, i: (b, 0, 0))
    const = lambda r, c: pl.BlockSpec((r, c), lambda b, i: (0, 0))
    return pl.pallas_call(
        kern,
        out_shape=(jax.ShapeDtypeStruct((bsz, seq, d), F32),
                   jax.ShapeDtypeStruct((bsz, seq, d), F32),
                   jax.ShapeDtypeStruct((bsz, seq, ROUTE_W), F32),
                   jax.ShapeDtypeStruct((1, LANES), F32)),
        grid=(bsz, seq // tm),
        in_specs=[tok(d), tok(lru_w), tok(att_w), const(lru_w + att_w, d), vec, vec, vec,
                  const(1, d), const(1, d), const(d, LANES), const(d, LANES), const(1, LANES)],
        out_specs=(tok(d), tok(d), tok(ROUTE_W), const(1, LANES)),
        scratch_shapes=[pltpu.VMEM((1, LANES), F32)],
        compiler_params=_cparams("arbitrary", "arbitrary"),
        name="mid_router",
    )(x, y_lru, y_att, w_out, g1, sh2, sc2, post_g, post_b, wr_hi, wr_lo, b_r)


def _row_copy(src, src_row, dst, dst_row, sem):
    return pltpu.make_async_copy(src.at[pl.ds(src_row, 1)], dst.at[pl.ds(dst_row, 1)], sem)


def _drain_rows(src, dst, sem, n):
    pltpu.make_async_copy(src.at[pl.ds(0, n)], dst.at[pl.ds(0, n)], sem).wait()


def _dispatch_kernel(slot_ref, h_ref, init_hbm, out_hbm, sem, *, tile):
    del init_hbm

    def issue(t, carry):
        for k in range(2):
            _row_copy(h_ref, t, out_hbm, slot_ref[0, 0, 2 * t + k], sem).start()
        return carry

    lax.fori_loop(0, tile, issue, 0, unroll=ISSUE_UNROLL)
    for _ in range(2):
        _drain_rows(h_ref, out_hbm, sem, tile)


def _dispatch(h2, slots, n_rows, *, tile):
    n, d = h2.shape
    n_steps = n // tile
    return pl.pallas_call(
        functools.partial(_dispatch_kernel, tile=tile),
        out_shape=jax.ShapeDtypeStruct((n_rows, d), F32),
        grid=(n_steps,),
        in_specs=[pl.BlockSpec((1, 1, 2 * tile), lambda i: (i, 0, 0), memory_space=pltpu.SMEM),
                  pl.BlockSpec((tile, d), lambda i: (i, 0)),
                  pl.BlockSpec(memory_space=pl.ANY)],
        out_specs=pl.BlockSpec(memory_space=pl.ANY),
        scratch_shapes=[pltpu.SemaphoreType.DMA(())],
        input_output_aliases={2: 0},
        compiler_params=_cparams("arbitrary"),
        name="moe_dispatch",
    )(slots.reshape(n_steps, 1, 2 * tile), h2, jnp.zeros((n_rows, d), F32))


def _experts_kernel(te_ref, nu_ref, h_ref, w1_ref, w3_ref, w2_ref, y_ref):
    del te_ref
    used = pl.program_id(0) < nu_ref[0]

    @pl.when(used)
    def _():
        h = h_ref[...].astype(BF16)
        a = _dot(h, w1_ref[0].astype(BF16))
        act = a * _sigmoid(a) * _dot(h, w3_ref[0].astype(BF16))
        y_ref[...] = _dot(act.astype(BF16), w2_ref[0].astype(BF16))

    @pl.when(jnp.logical_not(used))
    def _():
        y_ref[...] = jnp.zeros(y_ref.shape, F32)


def _experts(hs, tile_expert, n_used, w1, w3, w2, *, tile):
    n_rows, d = hs.shape
    d_e = w1.shape[2]
    used = lambda i, te, nu: jnp.minimum(i, nu[0] - 1)
    row_spec = pl.BlockSpec((tile, d), lambda i, te, nu: (used(i, te, nu), 0))
    w_spec = lambda r, c: pl.BlockSpec((1, r, c), lambda i, te, nu: (te[used(i, te, nu)], 0, 0))
    return pl.pallas_call(
        _experts_kernel,
        out_shape=jax.ShapeDtypeStruct((n_rows, d), F32),
        grid_spec=pltpu.PrefetchScalarGridSpec(
            num_scalar_prefetch=2,
            grid=(n_rows // tile,),
            in_specs=[row_spec, w_spec(d, d_e), w_spec(d, d_e), w_spec(d_e, d)],
            out_specs=pl.BlockSpec((tile, d), lambda i, te, nu: (i, 0))),
        compiler_params=_cparams("arbitrary"),
        name="moe_experts",
    )(tile_expert, n_used, hs, w1, w3, w2)


def _combine_kernel(slot_ref, next_slot_ref, route_ref, y_hbm, xmid_ref, g2_ref, pg_ref, pb_ref,
                    out_ref, ya_scr, yb_scr, sem, *, alpha, tile, n_steps):
    i = pl.program_id(0)
    cur = i % 2

    def gather(slots, buf):
        def issue(t, carry):
            _row_copy(y_hbm, slots[0, 0, 2 * t], ya_scr.at[buf], t, sem.at[buf]).start()
            _row_copy(y_hbm, slots[0, 0, 2 * t + 1], yb_scr.at[buf], t, sem.at[buf]).start()
            return carry
        lax.fori_loop(0, tile, issue, 0, unroll=ISSUE_UNROLL)

    @pl.when(i == 0)
    def _():
        gather(slot_ref, 0)

    @pl.when(i + 1 < n_steps)
    def _():
        gather(next_slot_ref, 1 - cur)

    _drain_rows(y_hbm, ya_scr.at[cur], sem.at[cur], tile)
    _drain_rows(y_hbm, yb_scr.at[cur], sem.at[cur], tile)
    route = route_ref[...]
    y = route[:, 2:3] * ya_scr[cur] + route[:, 3:4] * yb_scr[cur]
    z = alpha * xmid_ref[...] + g2_ref[0] * y
    out_ref[...] = _layer_norm(z) * pg_ref[...] + pb_ref[...]


def _combine(ys, slots, route, x_mid, g2, post_g, post_b, *, alpha, seq, tile):
    n, d = x_mid.shape
    n_steps = n // tile
    assert seq % tile == 0
    tok = lambda w: pl.BlockSpec((tile, w), lambda i: (i, 0))
    const = pl.BlockSpec((1, d), lambda i: (0, 0))
    slot_blocks = slots.reshape(n_steps, 1, 2 * tile)
    return pl.pallas_call(
        functools.partial(_combine_kernel, alpha=alpha, tile=tile, n_steps=n_steps),
        out_shape=jax.ShapeDtypeStruct((n, d), F32),
        grid=(n_steps,),
        in_specs=[pl.BlockSpec((1, 1, 2 * tile), lambda i: (i, 0, 0), memory_space=pltpu.SMEM),
                  pl.BlockSpec((1, 1, 2 * tile), lambda i: (jnp.minimum(i + 1, n_steps - 1), 0, 0),
                               memory_space=pltpu.SMEM),
                  tok(ROUTE_W),
                  pl.BlockSpec(memory_space=pl.ANY),
                  tok(d),
                  pl.BlockSpec((1, 1, d), lambda i: (i * tile // seq, 0, 0)),
                  const, const],
        out_specs=tok(d),
        scratch_shapes=[pltpu.VMEM((2, tile, d), F32), pltpu.VMEM((2, tile, d), F32),
                        pltpu.SemaphoreType.DMA((2,))],
        compiler_params=_cparams("arbitrary"),
        name="moe_combine",
    )(slot_blocks, slot_blocks, route, ys, x_mid, g2, post_g, post_b)


def _moe(h2, route, counts, w1, w3, w2, x_mid, g2, post_g, post_b, *, alpha):
    bsz, seq, d = x_mid.shape
    n_experts = w1.shape[0]
    n = bsz * seq
    tile_e = 256
    tile_t = min(seq, 512)
    n_tiles = 2 * n // tile_e + n_experts
    route = route.reshape(n, ROUTE_W)
    expert = route[:, 0:2].astype(jnp.int32)
    rank = route[:, 4:6].astype(jnp.int32)
    cnt = counts[0, :n_experts].astype(jnp.int32)
    tiles = (cnt + tile_e - 1) // tile_e
    tile_end = jnp.cumsum(tiles)
    slots = ((tile_end - tiles) * tile_e)[expert] + rank
    tile_ids = jnp.arange(n_tiles, dtype=jnp.int32)
    tile_expert = jnp.minimum(jnp.sum((tile_end[None, :] <= tile_ids[:, None]).astype(jnp.int32), axis=1),
                              n_experts - 1)
    n_used = tile_end[-1:].astype(jnp.int32)
    hs = _dispatch(h2.reshape(n, d), slots, n_tiles * tile_e, tile=tile_t)
    ys = _experts(hs, tile_expert, n_used, w1, w3, w2, tile=tile_e)
    out = _combine(ys, slots, route, x_mid.reshape(n, d), g2, post_g, post_b,
                   alpha=alpha, seq=seq, tile=tile_t)
    return out.reshape(bsz, seq, d)


def _block(x, c, ctx, c_ctx, w_mod, b_mod, w_in, conv_w, conv_b, lru_wa, lru_ba, lru_wi, lru_bi,
           lru_lambda, diff_lambda, attn_norm_g, w_out, post_g, post_b, router_g_w, router_g_b,
           router_e_w, router_e_b, exp_w1, exp_w3, exp_w2, *, grid_w):
    depth = w_mod.shape[0]
    assert depth == 1, "single-layer block only"
    bsz, seq, d = x.shape
    lru_w = conv_w.shape[-1]
    n_heads, dv = attn_norm_g.shape[1:]
    assert dv == HEAD_V_DIM and conv_w.shape[1] == CONV_W
    n_groups, _, per_group = router_e_w.shape[1:]
    assert n_groups + n_groups * per_group <= LANES
    alpha = (2.0 * depth) ** 0.25
    lam_init = 0.8 - 0.6 * math.exp(0.0)

    cond = jnp.concatenate([c, c_ctx[None], jnp.zeros((SUBLANES - 1 - bsz % SUBLANES, d), F32)], axis=0)
    mods = _mods(cond, w_mod[0], b_mod[0]).reshape(cond.shape[0], 6, 1, d)
    sh1, sc1, g1, sh2, sc2, g2 = (mods[:bsz, j] for j in range(6))
    sh1c, sc1c = (jnp.broadcast_to(mods[bsz:bsz + 1, j], (bsz, 1, d)) for j in range(2))

    w_in_b = w_in[0].astype(BF16)
    cos, sin = _rope_tables(seq, grid_w)
    u, gt, q, k, v = _inproj(x, sh1, sc1, w_in_b, cos, sin, lru_w=lru_w, n_heads=n_heads, rope=True)
    n_ctx = ctx.shape[1]
    u_c, _, _, k_c, v_c = _inproj(ctx, sh1c, sc1c, w_in_b, cos[:n_ctx], sin[:n_ctx],
                                  lru_w=lru_w, n_heads=n_heads, rope=False)

    zero_state = jnp.zeros((bsz, 1, lru_w), F32)
    cb = conv_b[0].reshape(1, lru_w)

    def gate_params(direction):
        w_gate = jnp.concatenate([_expand_block_diag(lru_wa[0, direction]),
                                  _expand_block_diag(lru_wi[0, direction])], axis=1).astype(BF16)
        b_gate = jnp.concatenate([lru_ba[0, direction], lru_bi[0, direction]]).reshape(1, 2 * lru_w)
        return w_gate, b_gate, lru_lambda[0, direction].reshape(1, lru_w)

    fwd_p, bwd_p = gate_params(0), gate_params(1)
    _, seed_f = _lru(u_c, conv_w[0], cb, *fwd_p, zero_state, reverse=False)
    _, seed_b = _lru(u_c, conv_w[0], cb, *bwd_p, zero_state, reverse=True)
    h_bwd, _ = _lru(u, conv_w[0], cb, *bwd_p, seed_b, reverse=True)
    y_lru, _ = _lru(u, conv_w[0], cb, *fwd_p, seed_f, reverse=False, other=h_bwd, gate=gt)

    y_att = _attention(diff_lambda[0], q, k, v, k_c, v_c, attn_norm_g[0].reshape(n_heads, 1, dv),
                       lam_init=lam_init)

    w_r = jnp.concatenate([router_g_w[0], jnp.moveaxis(router_e_w[0], 0, 1).reshape(d, -1)], axis=1)
    b_r = jnp.concatenate([router_g_b[0], router_e_b[0].reshape(-1)])
    pad = LANES - w_r.shape[1]
    w_r = jnp.pad(w_r, ((0, 0), (0, pad)))
    b_r = jnp.pad(b_r, (0, pad)).reshape(1, LANES)
    wr_hi, wr_lo = _split_bf16(w_r)

    x_mid, h2, route, counts = _mid(x, y_lru, y_att, w_out[0].astype(BF16), g1, sh2, sc2,
                                    post_g[0, 0].reshape(1, d), post_b[0, 0].reshape(1, d),
                                    wr_hi, wr_lo, b_r, alpha=alpha, n_groups=n_groups,
                                    per_group=per_group)
    return _moe(h2, route, counts, exp_w1[0], exp_w3[0], exp_w2[0], x_mid, g2,
                post_g[0, 1].reshape(1, d),
                post_b[0, 1].reshape(1, d), alpha=alpha)


def kernel(x, c, ctx, c_ctx, w_mod, b_mod, w_in, conv_w, conv_b, lru_wa, lru_ba, lru_wi, lru_bi, lru_lambda, diff_lambda, attn_norm_g, w_out, post_g, post_b, router_g_w, router_g_b, router_e_w, router_e_b, exp_w1, exp_w3, exp_w2):
    return _block(x, c, ctx, c_ctx, w_mod, b_mod, w_in, conv_w, conv_b, lru_wa, lru_ba, lru_wi,
                  lru_bi, lru_lambda, diff_lambda, attn_norm_g, w_out, post_g, post_b, router_g_w,
                  router_g_b, router_e_w, router_e_b, exp_w1, exp_w3, exp_w2, grid_w=GRID_W)
```

```python
import functools
import math

import jax
import jax.numpy as jnp
from jax import lax
from jax.experimental import pallas as pl
from jax.experimental.pallas import tpu as pltpu

F32 = jnp.float32
BF16 = jnp.bfloat16

LN_EPS = 1e-5
LRU_C = 8.0
ROPE_BASE = 10000.0
GRID_W = 64
HEAD_V_DIM = 128
HEAD_QK_DIM = HEAD_V_DIM // 2
CONV_W = 4
SUBLANES = 8
LANES = 128
NEG_BIG = -1e30
VMEM_LIMIT_BYTES = 56 * 1024 * 1024
ROUTE_W = 8
ISSUE_UNROLL = 8


def _cparams(*sem):
    return pltpu.CompilerParams(dimension_semantics=sem, vmem_limit_bytes=VMEM_LIMIT_BYTES)


def _layer_norm(x):
    mu = jnp.mean(x, axis=-1, keepdims=True)
    xc = x - mu
    var = jnp.mean(xc * xc, axis=-1, keepdims=True)
    return xc * lax.rsqrt(var + LN_EPS)


def _sigmoid(x):
    return 0.5 * jnp.tanh(0.5 * x) + 0.5


def _split_bf16(x):
    hi = x.astype(BF16)
    lo = (x - hi.astype(F32)).astype(BF16)
    return hi, lo


def _dot(a, b):
    return jnp.dot(a, b, preferred_element_type=F32)


def _dot_split(a_hi, a_lo, b_hi, b_lo):
    return _dot(a_hi, b_hi) + _dot(a_hi, b_lo) + _dot(a_lo, b_hi)


def _mods_kernel(c_ref, w_ref, b_ref, o_ref):
    c = c_ref[...]
    s = c * jax.nn.sigmoid(c)
    s_hi, s_lo = _split_bf16(s)
    w_hi, w_lo = _split_bf16(w_ref[...])
    o_ref[...] = _dot_split(s_hi, s_lo, w_hi, w_lo) + b_ref[...]


def _mods(cond, w, b):
    rows, d = cond.shape
    n = w.shape[1]
    tn = min(n, 1024)
    return pl.pallas_call(
        _mods_kernel,
        out_shape=jax.ShapeDtypeStruct((rows, n), F32),
        grid=(n // tn,),
        in_specs=[pl.BlockSpec((rows, d), lambda j: (0, 0)),
                  pl.BlockSpec((d, tn), lambda j: (0, j)),
                  pl.BlockSpec((1, tn), lambda j: (0, j))],
        out_specs=pl.BlockSpec((rows, tn), lambda j: (0, j)),
        compiler_params=_cparams("arbitrary"),
        name="mods",
    )(cond, w, b.reshape(1, n))


def _rope_tables(n_tokens, grid_w):
    n_freq = HEAD_QK_DIM // 4
    pos = jnp.arange(n_tokens, dtype=jnp.int32)
    row = (pos // grid_w).astype(F32)
    col = (pos % grid_w).astype(F32)
    inv = ROPE_BASE ** (-jnp.arange(n_freq, dtype=F32) / n_freq)
    ar = row[:, None] * inv
    ac = col[:, None] * inv
    cos64 = jnp.concatenate([jnp.cos(ar), jnp.cos(ar), jnp.cos(ac), jnp.cos(ac)], axis=1)
    sin64 = jnp.concatenate([-jnp.sin(ar), jnp.sin(ar), -jnp.sin(ac), jnp.sin(ac)], axis=1)
    reps = HEAD_V_DIM // HEAD_QK_DIM
    return jnp.tile(cos64, (1, reps)), jnp.tile(sin64, (1, reps))


def _inproj_kernel(x_ref, sh_ref, sc_ref, w_ref, cos_ref, sin_ref,
                   u_ref, g_ref, q_ref, k_ref, v_ref, *, lru_w, n_heads, rope):
    h = _layer_norm(x_ref[0]) * (1.0 + sc_ref[0]) + sh_ref[0]
    p = _dot(h.astype(BF16), w_ref[...])
    u_ref[0] = p[:, :lru_w]
    g_ref[0] = p[:, lru_w:2 * lru_w].astype(BF16)
    qk_w = n_heads * HEAD_V_DIM
    base_q = 2 * lru_w
    base_k = base_q + qk_w
    base_v = base_k + qk_w
    if rope:
        cos = cos_ref[...]
        sin = sin_ref[...]
        lane = lax.broadcasted_iota(jnp.int32, cos.shape, 1)
        first = (lane % 32) < 16
    q_scale = HEAD_QK_DIM ** -0.5 * math.log2(math.e)
    for hd in range(n_heads):
        lo, hi = hd * HEAD_V_DIM, (hd + 1) * HEAD_V_DIM
        for base, ref, scale in ((base_q, q_ref, q_scale), (base_k, k_ref, None)):
            t = p[:, base + lo:base + hi]
            if rope:
                partner = jnp.where(first, pltpu.roll(t, LANES - 16, 1), pltpu.roll(t, 16, 1))
                t = t * cos + partner * sin
            if scale is not None:
                t = t * scale
            ref[0, hd] = t.astype(BF16)
        v_ref[0, hd] = p[:, base_v + lo:base_v + hi].astype(BF16)


def _inproj(x, shift, scale, w_in, cos, sin, *, lru_w, n_heads, rope):
    bsz, seq, d = x.shape
    in_w = w_in.shape[1]
    tm = min(seq, 512)
    kern = functools.partial(_inproj_kernel, lru_w=lru_w, n_heads=n_heads, rope=rope)
    head_shape = jax.ShapeDtypeStruct((bsz, n_heads, seq, HEAD_V_DIM), BF16)
    head_spec = pl.BlockSpec((1, n_heads, tm, HEAD_V_DIM), lambda b, i: (b, 0, i, 0))
    vec_spec = pl.BlockSpec((1, 1, d), lambda b, i: (b, 0, 0))
    return pl.pallas_call(
        kern,
        out_shape=(jax.ShapeDtypeStruct((bsz, seq, lru_w), F32),
                   jax.ShapeDtypeStruct((bsz, seq, lru_w), BF16),
                   head_shape, head_shape, head_shape),
        grid=(bsz, seq // tm),
        in_specs=[pl.BlockSpec((1, tm, d), lambda b, i: (b, i, 0)),
                  vec_spec, vec_spec,
                  pl.BlockSpec((d, in_w), lambda b, i: (0, 0)),
                  pl.BlockSpec((tm, HEAD_V_DIM), lambda b, i: (i, 0)),
                  pl.BlockSpec((tm, HEAD_V_DIM), lambda b, i: (i, 0))],
        out_specs=(pl.BlockSpec((1, tm, lru_w), lambda b, i: (b, i, 0)),
                   pl.BlockSpec((1, tm, lru_w), lambda b, i: (b, i, 0)),
                   head_spec, head_spec, head_spec),
        compiler_params=_cparams("parallel", "arbitrary"),
        name="inproj_rope" if rope else "inproj_ctx",
    )(x, shift, scale, w_in, cos, sin)


def _scan_rows(a, b, reverse):
    row = lax.broadcasted_iota(jnp.int32, a.shape, 0)
    for s in (1, 2, 4):
        shift = SUBLANES - s if reverse else s
        a_sh = pltpu.roll(a, shift, 0)
        b_sh = pltpu.roll(b, shift, 0)
        valid = (row < SUBLANES - s) if reverse else (row >= s)
        b = jnp.where(valid, a * b_sh + b, b)
        a = jnp.where(valid, a * a_sh, a)
    return a, b


def _lru_kernel(*refs, reverse, combine, n_chunks, chunk):
    if combine:
        (u_ref, up_ref, un_ref, cw_ref, cb_ref, wg_ref, bg_ref, lam_ref, h0_ref, hb_ref, g_ref,
         out_ref, hn_ref, carry_ref, a_scr, b_scr) = refs
    else:
        (u_ref, up_ref, un_ref, cw_ref, cb_ref, wg_ref, bg_ref, lam_ref, h0_ref,
         out_ref, hn_ref, carry_ref, a_scr, b_scr) = refs
    c = pl.program_id(1)
    cc = (n_chunks - 1 - c) if reverse else c
    width = u_ref.shape[-1]

    @pl.when(c == 0)
    def _():
        carry_ref[...] = h0_ref[0]

    u = u_ref[0]
    prev = jnp.where(cc > 0, up_ref[0], 0.0)
    nxt = jnp.where(cc < n_chunks - 1, un_ref[0], 0.0)
    row = lax.broadcasted_iota(jnp.int32, u.shape, 0)
    u_m1 = jnp.where(row == 0, prev[7:8], pltpu.roll(u, 1, 0))
    u_m2 = jnp.where(row == 0, prev[6:7], jnp.where(row == 1, prev[7:8], pltpu.roll(u, 2, 0)))
    u_p1 = jnp.where(row == chunk - 1, nxt[0:1], pltpu.roll(u, chunk - 1, 0))
    cw = cw_ref[...]
    xc = cb_ref[...] + cw[0:1] * u_m2 + cw[1:2] * u_m1 + cw[2:3] * u + cw[3:4] * u_p1

    z = _dot(xc.astype(BF16), wg_ref[...]) + bg_ref[...]
    r = _sigmoid(z[:, :width])
    i = _sigmoid(z[:, width:])
    nlam = -lam_ref[...]
    softplus = jnp.maximum(nlam, 0.0) + jnp.log(1.0 + jnp.exp(-jnp.abs(nlam)))
    a = jnp.exp(-LRU_C * r * softplus)
    a_scr[...] = a
    gap = 1.0 - a * a
    b_scr[...] = jnp.where(gap > 0.0, gap * lax.rsqrt(gap), 0.0) * (i * xc)

    n_groups = chunk // SUBLANES

    def body(j, carry):
        jj = (n_groups - 1 - j) if reverse else j
        r0 = pl.multiple_of(jj * SUBLANES, SUBLANES)
        a_cum, h_loc = _scan_rows(a_scr[pl.ds(r0, SUBLANES), :], b_scr[pl.ds(r0, SUBLANES), :], reverse)
        h = a_cum * carry + h_loc
        b_scr[pl.ds(r0, SUBLANES), :] = h
        return h[0:1] if reverse else h[SUBLANES - 1:SUBLANES]

    carry = lax.fori_loop(0, n_groups, body, carry_ref[...])
    carry_ref[...] = carry
    hn_ref[0] = carry
    h_all = b_scr[...]
    if combine:
        out_ref[0] = (jax.nn.gelu(g_ref[0].astype(F32)) * (h_all + hb_ref[0])).astype(out_ref.dtype)
    else:
        out_ref[0] = h_all


def _lru(u, conv_w, conv_b, w_gate, b_gate, lam, h0, *, reverse, other=None, gate=None):
    bsz, seq, width = u.shape
    chunk = min(seq, 512)
    n_chunks = seq // chunk
    halo_per_chunk = chunk // SUBLANES
    n_halo = seq // SUBLANES
    combine = other is not None

    def pos(c):
        return (n_chunks - 1 - c) if reverse else c

    tile = lambda b, c: (b, pos(c), 0)
    prev_halo = lambda b, c: (b, jnp.maximum(pos(c) * halo_per_chunk - 1, 0), 0)
    next_halo = lambda b, c: (b, jnp.minimum((pos(c) + 1) * halo_per_chunk, n_halo - 1), 0)
    const2 = lambda b, c: (0, 0)
    state = lambda b, c: (b, 0, 0)
    in_specs = [pl.BlockSpec((1, chunk, width), tile),
                pl.BlockSpec((1, SUBLANES, width), prev_halo),
                pl.BlockSpec((1, SUBLANES, width), next_halo),
                pl.BlockSpec((CONV_W, width), const2),
                pl.BlockSpec((1, width), const2),
                pl.BlockSpec((width, 2 * width), const2),
                pl.BlockSpec((1, 2 * width), const2),
                pl.BlockSpec((1, width), const2),
                pl.BlockSpec((1, 1, width), state)]
    args = [u, u, u, conv_w, conv_b, w_gate, b_gate, lam, h0]
    if combine:
        in_specs += [pl.BlockSpec((1, chunk, width), tile), pl.BlockSpec((1, chunk, width), tile)]
        args += [other, gate]
    kern = functools.partial(_lru_kernel, reverse=reverse, combine=combine,
                             n_chunks=n_chunks, chunk=chunk)
    return pl.pallas_call(
        kern,
        out_shape=(jax.ShapeDtypeStruct((bsz, seq, width), BF16 if combine else F32),
                   jax.ShapeDtypeStruct((bsz, 1, width), F32)),
        grid=(bsz, n_chunks),
        in_specs=in_specs,
        out_specs=(pl.BlockSpec((1, chunk, width), tile), pl.BlockSpec((1, 1, width), state)),
        scratch_shapes=[pltpu.VMEM((1, width), F32),
                        pltpu.VMEM((chunk, width), F32),
                        pltpu.VMEM((chunk, width), F32)],
        compiler_params=_cparams("parallel", "arbitrary"),
        name=("lru_bwd" if reverse else "lru_fwd") + ("_mix" if combine else ""),
    )(*args)


def _expand_block_diag(w):
    n, k, _ = w.shape
    eye = jnp.eye(n, dtype=w.dtype)
    return (eye[:, None, :, None] * w[:, :, None, :]).reshape(n * k, n * k)


def _attn_kernel(dl_ref, q_ref, k_ref, v_ref, kc_ref, vc_ref, gn_ref, o_ref,
                 q2_scr, m_scr, acc_scr, s_scr, p_scr, a_scr,
                 *, tk, rb, unroll, lam_init):
    tq = q_ref.shape[2]
    dv = v_ref.shape[3]
    q = q_ref[0, 0]
    lane = lax.broadcasted_iota(jnp.int32, q.shape, 1)
    zero = jnp.zeros_like(q)
    q2_scr[:tq, :] = jnp.where(lane < HEAD_QK_DIM, q, zero)
    q2_scr[tq:, :] = jnp.where(lane >= HEAD_QK_DIM, q, zero)
    m_scr[...] = jnp.full(m_scr.shape, -jnp.inf, F32)
    acc_scr[...] = jnp.zeros(acc_scr.shape, F32)

    n_latent = k_ref.shape[2] // tk
    n_keys = k_ref.shape[2] + kc_ref.shape[2]
    n_chunks = n_latent + 1

    def chunk_len(j):
        return min(tk, n_keys - j * tk) if isinstance(j, int) else tk

    def kv_chunk(refs, j):
        latent, context = refs
        if isinstance(j, int):
            return context[0, 0] if j == n_latent else latent[0, 0, pl.ds(j * tk, tk), :]
        return latent[0, 0, pl.ds(pl.multiple_of(j * tk, tk), tk), :]

    keys = (k_ref, kc_ref)
    values = (v_ref, vc_ref)

    def put_scores(buf, j):
        s_scr[buf, :, pl.ds(0, chunk_len(j))] = lax.dot_general(
            q2_scr[...], kv_chunk(keys, j), (((1,), (1,)), ((), ())), preferred_element_type=F32)

    def softmax_rows(buf, n, part, parts):
        n_blocks = 2 * tq // rb
        for r in range(part * n_blocks // parts, (part + 1) * n_blocks // parts):
            rows = pl.ds(r * rb, rb)
            s = s_scr[buf, rows, pl.ds(0, n)]
            m_prev = m_scr[rows, :]
            m_new = jnp.maximum(m_prev, jnp.max(s, axis=-1, keepdims=True))
            p_scr[buf, rows, pl.ds(0, n)] = jnp.exp2(s - m_new).astype(BF16)
            a_scr[buf, rows, :] = jnp.exp2(m_prev - m_new)
            m_scr[rows, :] = m_new

    def accumulate(buf, j):
        n = chunk_len(j)
        ones = jnp.where(lax.broadcasted_iota(jnp.int32, (n, dv), 1) == 0, 1.0, 0.0).astype(BF16)
        v_ext = jnp.concatenate([kv_chunk(values, j), ones], axis=1)
        acc_scr[...] = a_scr[buf] * acc_scr[...] + _dot(p_scr[buf, :, pl.ds(0, n)], v_ext)

    put_scores(0, 0)
    p_scr[1] = jnp.zeros(p_scr.shape[1:], BF16)
    a_scr[1] = jnp.ones(a_scr.shape[1:], F32)

    def step(j, cur, with_scores):
        prev = max(j - 1, 0) if isinstance(j, int) else jnp.maximum(j - 1, 0)
        n = chunk_len(j)
        softmax_rows(cur, n, 0, 4)
        if with_scores:
            put_scores(1 - cur, j + 1)
        softmax_rows(cur, n, 1, 4)
        softmax_rows(cur, n, 2, 4)
        accumulate(1 - cur, prev)
        softmax_rows(cur, n, 3, 4)

    def body(jj, carry):
        for sub in range(unroll):
            step(unroll * jj + sub, sub % 2, True)
        return carry

    n_loops = (n_latent - 1) // unroll
    lax.fori_loop(0, n_loops, body, 0)
    for j in range(unroll * n_loops, n_chunks):
        step(j, j % 2, j + 1 < n_chunks)
    accumulate((n_chunks - 1) % 2, n_chunks - 1)

    lp = dl_ref[...]
    lam = (jnp.exp(jnp.sum(lp[0:1] * lp[1:2], axis=-1, keepdims=True))
           - jnp.exp(jnp.sum(lp[2:3] * lp[3:4], axis=-1, keepdims=True)) + lam_init)
    o_all = acc_scr[:, :dv] / acc_scr[:, dv:dv + 1]
    o = o_all[:tq] - lam * o_all[tq:]
    y = o * lax.rsqrt(jnp.mean(o * o, axis=-1, keepdims=True) + LN_EPS) * (1.0 - lam_init)
    o_ref[0] = (y * gn_ref[0]).astype(o_ref.dtype)


def _attention(diff_lambda, q, k, v, k_c, v_c, norm_g, *, lam_init):
    bsz, n_heads, seq, dv = q.shape
    n_ctx = k_c.shape[2]
    tq = min(seq, 512)
    tk = min(seq // 2, 512)
    rb = min(2 * tq, 32)
    assert seq % tk == 0 and n_ctx <= tk and n_ctx % LANES == 0
    unroll = 2
    kern = functools.partial(_attn_kernel, tk=tk, rb=rb, unroll=unroll, lam_init=lam_init)
    kv_spec = pl.BlockSpec((1, 1, seq, dv), lambda b, h, i: (b, h, 0, 0))
    ctx_spec = pl.BlockSpec((1, 1, n_ctx, dv), lambda b, h, i: (b, h, 0, 0))
    return pl.pallas_call(
        kern,
        out_shape=jax.ShapeDtypeStruct((bsz, seq, n_heads * dv), BF16),
        grid=(bsz, n_heads, seq // tq),
        in_specs=[pl.BlockSpec(diff_lambda.shape, lambda b, h, i: (0, 0)),
                  pl.BlockSpec((1, 1, tq, dv), lambda b, h, i: (b, h, i, 0)),
                  kv_spec, kv_spec, ctx_spec, ctx_spec,
                  pl.BlockSpec((1, 1, dv), lambda b, h, i: (h, 0, 0))],
        out_specs=pl.BlockSpec((1, tq, dv), lambda b, h, i: (b, i, h)),
        scratch_shapes=[pltpu.VMEM((2 * tq, dv), BF16),
                        pltpu.VMEM((2 * tq, 1), F32),
                        pltpu.VMEM((2 * tq, 2 * dv), F32),
                        pltpu.VMEM((2, 2 * tq, tk), F32),
                        pltpu.VMEM((2, 2 * tq, tk), BF16),
                        pltpu.VMEM((2, 2 * tq, 1), F32)],
        compiler_params=_cparams("parallel", "parallel", "arbitrary"),
        name="diff_attn",
    )(diff_lambda, q, k, v, k_c, v_c, norm_g)


def _mid_kernel(x_ref, yl_ref, ya_ref, wo_ref, g1_ref, sh2_ref, sc2_ref, pg_ref, pb_ref,
                wrh_ref, wrl_ref, br_ref, xmid_ref, h2_ref, route_ref, counts_ref, cnt_scr,
                *, alpha, n_groups, per_group):
    @pl.when((pl.program_id(0) == 0) & (pl.program_id(1) == 0))
    def _():
        cnt_scr[...] = jnp.zeros(cnt_scr.shape, F32)

    lru_w = yl_ref.shape[-1]
    mix = _dot(yl_ref[0], wo_ref[:lru_w, :]) + _dot(ya_ref[0], wo_ref[lru_w:, :])
    x_mid = _layer_norm(alpha * x_ref[0] + g1_ref[0] * mix) * pg_ref[...] + pb_ref[...]
    xmid_ref[0] = x_mid
    h2 = _layer_norm(x_mid) * (1.0 + sc2_ref[0]) + sh2_ref[0]
    h2_ref[0] = h2

    h_hi, h_lo = _split_bf16(h2)
    logits = _dot_split(h_hi, h_lo, wrh_ref[...], wrl_ref[...]) + br_ref[...]
    lane = lax.broadcasted_iota(jnp.int32, logits.shape, 1)

    def first_argmax(vals, vmax):
        return jnp.min(jnp.where(vals == vmax, lane, LANES), axis=-1, keepdims=True)

    gl = jnp.where(lane < n_groups, logits, NEG_BIG)
    g_max = jnp.max(gl, axis=-1, keepdims=True)
    p_g = 1.0 / jnp.sum(jnp.exp(gl - g_max), axis=-1, keepdims=True)
    g_sel = first_argmax(gl, g_max)
    e_lo = n_groups + g_sel * per_group
    el = jnp.where((lane >= e_lo) & (lane < e_lo + per_group), logits, NEG_BIG)
    m1 = jnp.max(el, axis=-1, keepdims=True)
    i1 = first_argmax(el, m1)
    el2 = jnp.where(lane == i1, NEG_BIG, el)
    m2 = jnp.max(el2, axis=-1, keepdims=True)
    i2 = first_argmax(el2, m2)
    e2 = jnp.exp(m2 - m1)
    w1 = 1.0 / (1.0 + e2)
    w2 = e2 * w1

    ex1 = i1 - n_groups
    ex2 = i2 - n_groups
    oh1 = jnp.where(lane == ex1, 1.0, 0.0)
    oh2 = jnp.where(lane == ex2, 1.0, 0.0)
    both = oh1 + oh2
    tm = both.shape[0]
    tri = (lax.broadcasted_iota(jnp.int32, (tm, tm), 0)
           > lax.broadcasted_iota(jnp.int32, (tm, tm), 1)).astype(BF16)
    before = _dot(tri, both.astype(BF16)) + cnt_scr[...]
    rank1 = jnp.sum(oh1 * before, axis=-1, keepdims=True)
    rank2 = jnp.sum(oh2 * before, axis=-1, keepdims=True)
    cnt_scr[...] += jnp.sum(both, axis=0, keepdims=True)
    counts_ref[...] = cnt_scr[...]

    fields = (ex1.astype(F32), ex2.astype(F32), p_g * w1, p_g * w2, rank1, rank2)
    slab = jnp.zeros(logits.shape, F32)
    for idx, val in enumerate(fields):
        slab = jnp.where(lane == idx, val, slab)
    route_ref[0] = slab[:, :ROUTE_W]


def _mid(x, y_lru, y_att, w_out, g1, sh2, sc2, post_g, post_b, wr_hi, wr_lo, b_r,
         *, alpha, n_groups, per_group):
    bsz, seq, d = x.shape
    lru_w = y_lru.shape[-1]
    att_w = y_att.shape[-1]
    tm = min(seq, 512)
    kern = functools.partial(_mid_kernel, alpha=alpha, n_groups=n_groups, per_group=per_group)
    tok = lambda w: pl.BlockSpec((1, tm, w), lambda b, i: (b, i, 0))
    vec = pl.BlockSpec((1, 1, d), lambda b, i: (b, 0, 0))
    const = lambda r, c: pl.BlockSpec((r, c), lambda b, i: (0, 0))
    return pl.pallas_call(
        kern,
        out_shape=(jax.ShapeDtypeStruct((bsz, seq, d), F32),
                   jax.ShapeDtypeStruct((bsz, seq, d), F32),
                   jax.ShapeDtypeStruct((bsz, seq, ROUTE_W), F32),
                   jax.ShapeDtypeStruct((1, LANES), F32)),
        grid=(bsz, seq // tm),
        in_specs=[tok(d), tok(lru_w), tok(att_w), const(lru_w + att_w, d), vec, vec, vec,
                  const(1, d), const(1, d), const(d, LANES), const(d, LANES), const(1, LANES)],
        out_specs=(tok(d), tok(d), tok(ROUTE_W), const(1, LANES)),
        scratch_shapes=[pltpu.VMEM((1, LANES), F32)],
        compiler_params=_cparams("arbitrary", "arbitrary"),
        name="mid_router",
    )(x, y_lru, y_att, w_out, g1, sh2, sc2, post_g, post_b, wr_hi, wr_lo, b_r)


def _row_copy(src, src_row, dst, dst_row, sem):
    return pltpu.make_async_copy(src.at[pl.ds(src_row, 1)], dst.at[pl.ds(dst_row, 1)], sem)


def _drain_rows(src, dst, sem, n):
    pltpu.make_async_copy(src.at[pl.ds(0, n)], dst.at[pl.ds(0, n)], sem).wait()


def _dispatch_kernel(te_ref, nu_ref, slot_ref, h_ref, out_hbm, zero_scr, sem, zero_sem,
                     *, tile, tile_e, n_tiles, n_experts):
    def zero_tile(t):
        dst = out_hbm.at[pl.ds(pl.multiple_of(t * tile_e, tile_e), tile_e)]
        return pltpu.make_async_copy(zero_scr, dst, zero_sem)

    def has_tiles(e):
        return te_ref[e] > (te_ref[e - 1] if e > 0 else 0)

    @pl.when(pl.program_id(0) == 0)
    def _():
        zero_scr[...] = jnp.zeros(zero_scr.shape, F32)
        for e in range(n_experts):
            @pl.when(has_tiles(e))
            def _():
                zero_tile(te_ref[e] - 1).start()
        lax.fori_loop(nu_ref[0], n_tiles, lambda t, c: (zero_tile(t).start(), c)[1], 0)
        for e in range(n_experts):
            @pl.when(has_tiles(e))
            def _():
                zero_tile(0).wait()
        lax.fori_loop(nu_ref[0], n_tiles, lambda t, c: (zero_tile(0).wait(), c)[1], 0)

    def issue(t, carry):
        for k in range(2):
            _row_copy(h_ref, t, out_hbm, slot_ref[0, 0, k * tile + t], sem).start()
        return carry

    lax.fori_loop(0, tile, issue, 0, unroll=ISSUE_UNROLL)
    for _ in range(2):
        _drain_rows(h_ref, out_hbm, sem, tile)


def _dispatch(h2, slots, tile_end, n_used, n_tiles, *, tile, tile_e):
    n, d = h2.shape
    kern = functools.partial(_dispatch_kernel, tile=tile, tile_e=tile_e, n_tiles=n_tiles,
                             n_experts=tile_end.shape[0])
    return pl.pallas_call(
        kern,
        out_shape=jax.ShapeDtypeStruct((n_tiles * tile_e, d), F32),
        grid_spec=pltpu.PrefetchScalarGridSpec(
            num_scalar_prefetch=2,
            grid=(n // tile,),
            in_specs=[pl.BlockSpec((1, 1, 2 * tile), lambda i, te, nu: (i, 0, 0),
                                   memory_space=pltpu.SMEM),
                      pl.BlockSpec((tile, d), lambda i, te, nu: (i, 0))],
            out_specs=pl.BlockSpec(memory_space=pl.ANY),
            scratch_shapes=[pltpu.VMEM((tile_e, d), F32), pltpu.SemaphoreType.DMA(()),
                            pltpu.SemaphoreType.DMA(())]),
        compiler_params=_cparams("arbitrary"),
        name="moe_dispatch",
    )(tile_end, n_used, slots, h2)


def _experts_kernel(te_ref, nu_ref, h_ref, w1_ref, w3_ref, w2_ref, y_ref):
    del te_ref
    used = pl.program_id(0) < nu_ref[0]

    @pl.when(used)
    def _():
        h = h_ref[...].astype(BF16)
        a = _dot(h, w1_ref[0].astype(BF16))
        act = a * _sigmoid(a) * _dot(h, w3_ref[0].astype(BF16))
        y_ref[...] = _dot(act.astype(BF16), w2_ref[0].astype(BF16))

    @pl.when(jnp.logical_not(used))
    def _():
        y_ref[...] = jnp.zeros(y_ref.shape, F32)


def _experts(hs, tile_expert, n_used, w1, w3, w2, *, tile):
    n_rows, d = hs.shape
    d_e = w1.shape[2]
    used = lambda i, te, nu: jnp.minimum(i, nu[0] - 1)
    row_spec = pl.BlockSpec((tile, d), lambda i, te, nu: (used(i, te, nu), 0))
    w_spec = lambda r, c: pl.BlockSpec((1, r, c), lambda i, te, nu: (te[used(i, te, nu)], 0, 0))
    return pl.pallas_call(
        _experts_kernel,
        out_shape=jax.ShapeDtypeStruct((n_rows, d), F32),
        grid_spec=pltpu.PrefetchScalarGridSpec(
            num_scalar_prefetch=2,
            grid=(n_rows // tile,),
            in_specs=[row_spec, w_spec(d, d_e), w_spec(d, d_e), w_spec(d_e, d)],
            out_specs=pl.BlockSpec((tile, d), lambda i, te, nu: (i, 0))),
        compiler_params=_cparams("arbitrary"),
        name="moe_experts",
    )(tile_expert, n_used, hs, w1, w3, w2)


def _combine_kernel(slot_ref, next_slot_ref, route_ref, y_hbm, xmid_ref, g2_ref, pg_ref, pb_ref,
                    out_ref, ya_scr, yb_scr, sem, *, alpha, tile, n_steps):
    i = pl.program_id(0)
    cur = i % 2

    def gather(slots, buf):
        def issue(t, carry):
            _row_copy(y_hbm, slots[0, 0, t], ya_scr.at[buf], t, sem.at[buf]).start()
            _row_copy(y_hbm, slots[0, 0, tile + t], yb_scr.at[buf], t, sem.at[buf]).start()
            return carry
        lax.fori_loop(0, tile, issue, 0, unroll=ISSUE_UNROLL)

    @pl.when(i == 0)
    def _():
        gather(slot_ref, 0)

    @pl.when(i + 1 < n_steps)
    def _():
        gather(next_slot_ref, 1 - cur)

    _drain_rows(y_hbm, ya_scr.at[cur], sem.at[cur], tile)
    _drain_rows(y_hbm, yb_scr.at[cur], sem.at[cur], tile)
    route = route_ref[...]
    y = route[:, 2:3] * ya_scr[cur] + route[:, 3:4] * yb_scr[cur]
    z = alpha * xmid_ref[...] + g2_ref[0] * y
    out_ref[...] = _layer_norm(z) * pg_ref[...] + pb_ref[...]


def _combine(ys, slots, route, x_mid, g2, post_g, post_b, *, alpha, seq, tile):
    n, d = x_mid.shape
    n_steps = n // tile
    assert seq % tile == 0
    tok = lambda w: pl.BlockSpec((tile, w), lambda i: (i, 0))
    const = pl.BlockSpec((1, d), lambda i: (0, 0))
    return pl.pallas_call(
        functools.partial(_combine_kernel, alpha=alpha, tile=tile, n_steps=n_steps),
        out_shape=jax.ShapeDtypeStruct((n, d), F32),
        grid=(n_steps,),
        in_specs=[pl.BlockSpec((1, 1, 2 * tile), lambda i: (i, 0, 0), memory_space=pltpu.SMEM),
                  pl.BlockSpec((1, 1, 2 * tile), lambda i: (jnp.minimum(i + 1, n_steps - 1), 0, 0),
                               memory_space=pltpu.SMEM),
                  tok(ROUTE_W),
                  pl.BlockSpec(memory_space=pl.ANY),
                  tok(d),
                  pl.BlockSpec((1, 1, d), lambda i: (i * tile // seq, 0, 0)),
                  const, const],
        out_specs=tok(d),
        scratch_shapes=[pltpu.VMEM((2, tile, d), F32), pltpu.VMEM((2, tile, d), F32),
                        pltpu.SemaphoreType.DMA((2,))],
        compiler_params=_cparams("arbitrary"),
        name="moe_combine",
    )(slots, slots, route, ys, x_mid, g2, post_g, post_b)


def _moe(h2, route, counts, w1, w3, w2, x_mid, g2, post_g, post_b, *, alpha):
    bsz, seq, d = x_mid.shape
    n_experts = w1.shape[0]
    n = bsz * seq
    tile_e = 256
    tile_t = min(seq, 512)
    n_tiles = 2 * n // tile_e + n_experts
    route = route.reshape(n, ROUTE_W)
    route_t = route.T
    expert = route_t[0:2].astype(jnp.int32)
    rank = route_t[4:6].astype(jnp.int32)
    cnt = counts[0, :n_experts].astype(jnp.int32)
    tiles = (cnt + tile_e - 1) // tile_e
    tile_end = jnp.cumsum(tiles)
    first_row = (tile_end - tiles) * tile_e
    onehot = expert[None] == jnp.arange(n_experts, dtype=jnp.int32)[:, None, None]
    slots = jnp.sum(jnp.where(onehot, first_row[:, None, None], 0), axis=0) + rank
    slots = slots.reshape(2, n // tile_t, tile_t).transpose(1, 0, 2).reshape(n // tile_t, 1, 2 * tile_t)
    tile_ids = jnp.arange(n_tiles, dtype=jnp.int32)
    tile_expert = jnp.minimum(jnp.sum((tile_end[None, :] <= tile_ids[:, None]).astype(jnp.int32), axis=1),
                              n_experts - 1)
    n_used = tile_end[-1:].astype(jnp.int32)
    hs = _dispatch(h2.reshape(n, d), slots, tile_end.astype(jnp.int32), n_used, n_tiles,
                   tile=tile_t, tile_e=tile_e)
    ys = _experts(hs, tile_expert, n_used, w1, w3, w2, tile=tile_e)
    out = _combine(ys, slots, route, x_mid.reshape(n, d), g2, post_g, post_b,
                   alpha=alpha, seq=seq, tile=tile_t)
    return out.reshape(bsz, seq, d)


def _block(x, c, ctx, c_ctx, w_mod, b_mod, w_in, conv_w, conv_b, lru_wa, lru_ba, lru_wi, lru_bi,
           lru_lambda, diff_lambda, attn_norm_g, w_out, post_g, post_b, router_g_w, router_g_b,
           router_e_w, router_e_b, exp_w1, exp_w3, exp_w2, *, grid_w):
    depth = w_mod.shape[0]
    assert depth == 1, "single-layer block only"
    bsz, seq, d = x.shape
    lru_w = conv_w.shape[-1]
    n_heads, dv = attn_norm_g.shape[1:]
    assert dv == HEAD_V_DIM and conv_w.shape[1] == CONV_W
    n_groups, _, per_group = router_e_w.shape[1:]
    assert n_groups + n_groups * per_group <= LANES
    alpha = (2.0 * depth) ** 0.25
    lam_init = 0.8 - 0.6 * math.exp(0.0)

    cond = jnp.concatenate([c, c_ctx[None], jnp.zeros((SUBLANES - 1 - bsz % SUBLANES, d), F32)], axis=0)
    mods = _mods(cond, w_mod[0], b_mod[0]).reshape(cond.shape[0], 6, 1, d)
    sh1, sc1, g1, sh2, sc2, g2 = (mods[:bsz, j] for j in range(6))
    sh1c, sc1c = (jnp.broadcast_to(mods[bsz:bsz + 1, j], (bsz, 1, d)) for j in range(2))

    w_in_b = w_in[0].astype(BF16)
    cos, sin = _rope_tables(seq, grid_w)
    u, gt, q, k, v = _inproj(x, sh1, sc1, w_in_b, cos, sin, lru_w=lru_w, n_heads=n_heads, rope=True)
    n_ctx = ctx.shape[1]
    u_c, _, _, k_c, v_c = _inproj(ctx, sh1c, sc1c, w_in_b, cos[:n_ctx], sin[:n_ctx],
                                  lru_w=lru_w, n_heads=n_heads, rope=False)

    zero_state = jnp.zeros((bsz, 1, lru_w), F32)
    cb = conv_b[0].reshape(1, lru_w)

    def gate_params(direction):
        w_gate = jnp.concatenate([_expand_block_diag(lru_wa[0, direction]),
                                  _expand_block_diag(lru_wi[0, direction])], axis=1).astype(BF16)
        b_gate = jnp.concatenate([lru_ba[0, direction], lru_bi[0, direction]]).reshape(1, 2 * lru_w)
        return w_gate, b_gate, lru_lambda[0, direction].reshape(1, lru_w)

    fwd_p, bwd_p = gate_params(0), gate_params(1)
    _, seed_f = _lru(u_c, conv_w[0], cb, *fwd_p, zero_state, reverse=False)
    _, seed_b = _lru(u_c, conv_w[0], cb, *bwd_p, zero_state, reverse=True)
    h_bwd, _ = _lru(u, conv_w[0], cb, *bwd_p, seed_b, reverse=True)
    y_lru, _ = _lru(u, conv_w[0], cb, *fwd_p, seed_f, reverse=False, other=h_bwd, gate=gt)

    y_att = _attention(diff_lambda[0], q, k, v, k_c, v_c, attn_norm_g[0].reshape(n_heads, 1, dv),
                       lam_init=lam_init)

    w_r = jnp.concatenate([router_g_w[0], jnp.moveaxis(router_e_w[0], 0, 1).reshape(d, -1)], axis=1)
    b_r = jnp.concatenate([router_g_b[0], router_e_b[0].reshape(-1)])
    pad = LANES - w_r.shape[1]
    w_r = jnp.pad(w_r, ((0, 0), (0, pad)))
    b_r = jnp.pad(b_r, (0, pad)).reshape(1, LANES)
    wr_hi, wr_lo = _split_bf16(w_r)

    x_mid, h2, route, counts = _mid(x, y_lru, y_att, w_out[0].astype(BF16), g1, sh2, sc2,
                                    post_g[0, 0].reshape(1, d), post_b[0, 0].reshape(1, d),
                                    wr_hi, wr_lo, b_r, alpha=alpha, n_groups=n_groups,
                                    per_group=per_group)
    return _moe(h2, route, counts, exp_w1[0], exp_w3[0], exp_w2[0], x_mid, g2,
                post_g[0, 1].reshape(1, d),
                post_b[0, 1].reshape(1, d), alpha=alpha)


def kernel(x, c, ctx, c_ctx, w_mod, b_mod, w_in, conv_w, conv_b, lru_wa, lru_ba, lru_wi, lru_bi, lru_lambda, diff_lambda, attn_norm_g, w_out, post_g, post_b, router_g_w, router_g_b, router_e_w, router_e_b, exp_w1, exp_w3, exp_w2):
    return _block(x, c, ctx, c_ctx, w_mod, b_mod, w_in, conv_w, conv_b, lru_wa, lru_ba, lru_wi,
                  lru_bi, lru_lambda, diff_lambda, attn_norm_g, w_out, post_g, post_b, router_g_w,
                  router_g_b, router_e_w, router_e_b, exp_w1, exp_w3, exp_w2, grid_w=GRID_W)
```

```python
import functools
import math

import jax
import jax.numpy as jnp
from jax import lax
from jax.experimental import pallas as pl
from jax.experimental.pallas import tpu as pltpu

F32 = jnp.float32
BF16 = jnp.bfloat16

LN_EPS = 1e-5
LRU_C = 8.0
ROPE_BASE = 10000.0
GRID_W = 64
HEAD_V_DIM = 128
HEAD_QK_DIM = HEAD_V_DIM // 2
CONV_W = 4
SUBLANES = 8
LANES = 128
NEG_BIG = -1e30
VMEM_LIMIT_BYTES = 56 * 1024 * 1024
ROUTE_W = 8
ISSUE_UNROLL = 8


def _cparams(*sem):
    return pltpu.CompilerParams(dimension_semantics=sem, vmem_limit_bytes=VMEM_LIMIT_BYTES)


def _layer_norm(x):
    mu = jnp.mean(x, axis=-1, keepdims=True)
    xc = x - mu
    var = jnp.mean(xc * xc, axis=-1, keepdims=True)
    return xc * lax.rsqrt(var + LN_EPS)


def _sigmoid(x):
    return 0.5 * jnp.tanh(0.5 * x) + 0.5


def _split_bf16(x):
    hi = x.astype(BF16)
    lo = (x - hi.astype(F32)).astype(BF16)
    return hi, lo


def _dot(a, b):
    return jnp.dot(a, b, preferred_element_type=F32)


def _dot_split(a_hi, a_lo, b_hi, b_lo):
    return _dot(a_hi, b_hi) + _dot(a_hi, b_lo) + _dot(a_lo, b_hi)


def _mods_kernel(c_ref, w_ref, b_ref, o_ref):
    c = c_ref[...]
    s = c * jax.nn.sigmoid(c)
    s_hi, s_lo = _split_bf16(s)
    w_hi, w_lo = _split_bf16(w_ref[...])
    o_ref[...] = _dot_split(s_hi, s_lo, w_hi, w_lo) + b_ref[...]


def _mods(cond, w, b):
    rows, d = cond.shape
    n = w.shape[1]
    tn = min(n, 1024)
    return pl.pallas_call(
        _mods_kernel,
        out_shape=jax.ShapeDtypeStruct((rows, n), F32),
        grid=(n // tn,),
        in_specs=[pl.BlockSpec((rows, d), lambda j: (0, 0)),
                  pl.BlockSpec((d, tn), lambda j: (0, j)),
                  pl.BlockSpec((1, tn), lambda j: (0, j))],
        out_specs=pl.BlockSpec((rows, tn), lambda j: (0, j)),
        compiler_params=_cparams("arbitrary"),
        name="mods",
    )(cond, w, b.reshape(1, n))


def _rope_tables(n_tokens, grid_w):
    n_freq = HEAD_QK_DIM // 4
    pos = jnp.arange(n_tokens, dtype=jnp.int32)
    row = (pos // grid_w).astype(F32)
    col = (pos % grid_w).astype(F32)
    inv = ROPE_BASE ** (-jnp.arange(n_freq, dtype=F32) / n_freq)
    ar = row[:, None] * inv
    ac = col[:, None] * inv
    cos64 = jnp.concatenate([jnp.cos(ar), jnp.cos(ar), jnp.cos(ac), jnp.cos(ac)], axis=1)
    sin64 = jnp.concatenate([-jnp.sin(ar), jnp.sin(ar), -jnp.sin(ac), jnp.sin(ac)], axis=1)
    reps = HEAD_V_DIM // HEAD_QK_DIM
    return jnp.tile(cos64, (1, reps)), jnp.tile(sin64, (1, reps))


def _inproj_kernel(x_ref, sh_ref, sc_ref, w_ref, cos_ref, sin_ref,
                   u_ref, g_ref, q_ref, k_ref, v_ref, *, lru_w, n_heads, rope):
    h = _layer_norm(x_ref[0]) * (1.0 + sc_ref[0]) + sh_ref[0]
    p = _dot(h.astype(BF16), w_ref[...])
    u_ref[0] = p[:, :lru_w]
    g_ref[0] = p[:, lru_w:2 * lru_w].astype(BF16)
    qk_w = n_heads * HEAD_V_DIM
    base_q = 2 * lru_w
    base_k = base_q + qk_w
    base_v = base_k + qk_w
    if rope:
        cos = cos_ref[...]
        sin = sin_ref[...]
        lane = lax.broadcasted_iota(jnp.int32, cos.shape, 1)
        first = (lane % 32) < 16
    q_scale = HEAD_QK_DIM ** -0.5 * math.log2(math.e)
    for hd in range(n_heads):
        lo, hi = hd * HEAD_V_DIM, (hd + 1) * HEAD_V_DIM
        for base, ref, scale in ((base_q, q_ref, q_scale), (base_k, k_ref, None)):
            t = p[:, base + lo:base + hi]
            if rope:
                partner = jnp.where(first, pltpu.roll(t, LANES - 16, 1), pltpu.roll(t, 16, 1))
                t = t * cos + partner * sin
            if scale is not None:
                t = t * scale
            ref[0, hd] = t.astype(BF16)
        v_ref[0, hd] = p[:, base_v + lo:base_v + hi].astype(BF16)


def _inproj(x, shift, scale, w_in, cos, sin, *, lru_w, n_heads, rope):
    bsz, seq, d = x.shape
    in_w = w_in.shape[1]
    tm = min(seq, 512)
    kern = functools.partial(_inproj_kernel, lru_w=lru_w, n_heads=n_heads, rope=rope)
    head_shape = jax.ShapeDtypeStruct((bsz, n_heads, seq, HEAD_V_DIM), BF16)
    head_spec = pl.BlockSpec((1, n_heads, tm, HEAD_V_DIM), lambda b, i: (b, 0, i, 0))
    vec_spec = pl.BlockSpec((1, 1, d), lambda b, i: (b, 0, 0))
    return pl.pallas_call(
        kern,
        out_shape=(jax.ShapeDtypeStruct((bsz, seq, lru_w), F32),
                   jax.ShapeDtypeStruct((bsz, seq, lru_w), BF16),
                   head_shape, head_shape, head_shape),
        grid=(bsz, seq // tm),
        in_specs=[pl.BlockSpec((1, tm, d), lambda b, i: (b, i, 0)),
                  vec_spec, vec_spec,
                  pl.BlockSpec((d, in_w), lambda b, i: (0, 0)),
                  pl.BlockSpec((tm, HEAD_V_DIM), lambda b, i: (i, 0)),
                  pl.BlockSpec((tm, HEAD_V_DIM), lambda b, i: (i, 0))],
        out_specs=(pl.BlockSpec((1, tm, lru_w), lambda b, i: (b, i, 0)),
                   pl.BlockSpec((1, tm, lru_w), lambda b, i: (b, i, 0)),
                   head_spec, head_spec, head_spec),
        compiler_params=_cparams("parallel", "arbitrary"),
        name="inproj_rope" if rope else "inproj_ctx",
    )(x, shift, scale, w_in, cos, sin)


def _scan_rows(a, b, reverse):
    row = lax.broadcasted_iota(jnp.int32, a.shape, 0)
    for s in (1, 2, 4):
        shift = SUBLANES - s if reverse else s
        a_sh = pltpu.roll(a, shift, 0)
        b_sh = pltpu.roll(b, shift, 0)
        valid = (row < SUBLANES - s) if reverse else (row >= s)
        b = jnp.where(valid, a * b_sh + b, b)
        a = jnp.where(valid, a * a_sh, a)
    return a, b


def _lru_kernel(*refs, reverse, combine, n_chunks, chunk):
    if combine:
        (u_ref, up_ref, un_ref, cw_ref, cb_ref, wg_ref, bg_ref, lam_ref, h0_ref, hb_ref, g_ref,
         out_ref, hn_ref, carry_ref, a_scr, b_scr) = refs
    else:
        (u_ref, up_ref, un_ref, cw_ref, cb_ref, wg_ref, bg_ref, lam_ref, h0_ref,
         out_ref, hn_ref, carry_ref, a_scr, b_scr) = refs
    c = pl.program_id(1)
    cc = (n_chunks - 1 - c) if reverse else c
    width = u_ref.shape[-1]

    @pl.when(c == 0)
    def _():
        carry_ref[...] = h0_ref[0]

    u = u_ref[0]
    prev = jnp.where(cc > 0, up_ref[0], 0.0)
    nxt = jnp.where(cc < n_chunks - 1, un_ref[0], 0.0)
    row = lax.broadcasted_iota(jnp.int32, u.shape, 0)
    u_m1 = jnp.where(row == 0, prev[7:8], pltpu.roll(u, 1, 0))
    u_m2 = jnp.where(row == 0, prev[6:7], jnp.where(row == 1, prev[7:8], pltpu.roll(u, 2, 0)))
    u_p1 = jnp.where(row == chunk - 1, nxt[0:1], pltpu.roll(u, chunk - 1, 0))
    cw = cw_ref[...]
    xc = cb_ref[...] + cw[0:1] * u_m2 + cw[1:2] * u_m1 + cw[2:3] * u + cw[3:4] * u_p1

    z = _dot(xc.astype(BF16), wg_ref[...]) + bg_ref[...]
    r = _sigmoid(z[:, :width])
    i = _sigmoid(z[:, width:])
    nlam = -lam_ref[...]
    softplus = jnp.maximum(nlam, 0.0) + jnp.log(1.0 + jnp.exp(-jnp.abs(nlam)))
    a = jnp.exp(-LRU_C * r * softplus)
    a_scr[...] = a
    gap = 1.0 - a * a
    b_scr[...] = jnp.where(gap > 0.0, gap * lax.rsqrt(gap), 0.0) * (i * xc)

    n_groups = chunk // SUBLANES

    def body(j, carry):
        jj = (n_groups - 1 - j) if reverse else j
        r0 = pl.multiple_of(jj * SUBLANES, SUBLANES)
        a_cum, h_loc = _scan_rows(a_scr[pl.ds(r0, SUBLANES), :], b_scr[pl.ds(r0, SUBLANES), :], reverse)
        h = a_cum * carry + h_loc
        b_scr[pl.ds(r0, SUBLANES), :] = h
        return h[0:1] if reverse else h[SUBLANES - 1:SUBLANES]

    carry = lax.fori_loop(0, n_groups, body, carry_ref[...])
    carry_ref[...] = carry
    hn_ref[0] = carry
    h_all = b_scr[...]
    if combine:
        out_ref[0] = (jax.nn.gelu(g_ref[0].astype(F32)) * (h_all + hb_ref[0])).astype(out_ref.dtype)
    else:
        out_ref[0] = h_all


def _lru(u, conv_w, conv_b, w_gate, b_gate, lam, h0, *, reverse, other=None, gate=None):
    bsz, seq, width = u.shape
    chunk = min(seq, 512)
    n_chunks = seq // chunk
    halo_per_chunk = chunk // SUBLANES
    n_halo = seq // SUBLANES
    combine = other is not None

    def pos(c):
        return (n_chunks - 1 - c) if reverse else c

    tile = lambda b, c: (b, pos(c), 0)
    prev_halo = lambda b, c: (b, jnp.maximum(pos(c) * halo_per_chunk - 1, 0), 0)
    next_halo = lambda b, c: (b, jnp.minimum((pos(c) + 1) * halo_per_chunk, n_halo - 1), 0)
    const2 = lambda b, c: (0, 0)
    state = lambda b, c: (b, 0, 0)
    in_specs = [pl.BlockSpec((1, chunk, width), tile),
                pl.BlockSpec((1, SUBLANES, width), prev_halo),
                pl.BlockSpec((1, SUBLANES, width), next_halo),
                pl.BlockSpec((CONV_W, width), const2),
                pl.BlockSpec((1, width), const2),
                pl.BlockSpec((width, 2 * width), const2),
                pl.BlockSpec((1, 2 * width), const2),
                pl.BlockSpec((1, width), const2),
                pl.BlockSpec((1, 1, width), state)]
    args = [u, u, u, conv_w, conv_b, w_gate, b_gate, lam, h0]
    if combine:
        in_specs += [pl.BlockSpec((1, chunk, width), tile), pl.BlockSpec((1, chunk, width), tile)]
        args += [other, gate]
    kern = functools.partial(_lru_kernel, reverse=reverse, combine=combine,
                             n_chunks=n_chunks, chunk=chunk)
    return pl.pallas_call(
        kern,
        out_shape=(jax.ShapeDtypeStruct((bsz, seq, width), BF16 if combine else F32),
                   jax.ShapeDtypeStruct((bsz, 1, width), F32)),
        grid=(bsz, n_chunks),
        in_specs=in_specs,
        out_specs=(pl.BlockSpec((1, chunk, width), tile), pl.BlockSpec((1, 1, width), state)),
        scratch_shapes=[pltpu.VMEM((1, width), F32),
                        pltpu.VMEM((chunk, width), F32),
                        pltpu.VMEM((chunk, width), F32)],
        compiler_params=_cparams("parallel", "arbitrary"),
        name=("lru_bwd" if reverse else "lru_fwd") + ("_mix" if combine else ""),
    )(*args)


def _expand_block_diag(w):
    n, k, _ = w.shape
    eye = jnp.eye(n, dtype=w.dtype)
    return (eye[:, None, :, None] * w[:, :, None, :]).reshape(n * k, n * k)


def _attn_kernel(dl_ref, q_ref, k_ref, v_ref, kc_ref, vc_ref, gn_ref, o_ref,
                 q2_scr, m_scr, acc_scr, s_scr, p_scr, a_scr,
                 *, tk, rb, unroll, lam_init):
    tq = q_ref.shape[2]
    dv = v_ref.shape[3]
    q = q_ref[0, 0]
    lane = lax.broadcasted_iota(jnp.int32, q.shape, 1)
    zero = jnp.zeros_like(q)
    q2_scr[:tq, :] = jnp.where(lane < HEAD_QK_DIM, q, zero)
    q2_scr[tq:, :] = jnp.where(lane >= HEAD_QK_DIM, q, zero)
    m_scr[...] = jnp.full(m_scr.shape, -jnp.inf, F32)
    acc_scr[...] = jnp.zeros(acc_scr.shape, F32)

    n_latent = k_ref.shape[2] // tk
    n_keys = k_ref.shape[2] + kc_ref.shape[2]
    n_chunks = n_latent + 1

    def chunk_len(j):
        return min(tk, n_keys - j * tk) if isinstance(j, int) else tk

    def kv_chunk(refs, j):
        latent, context = refs
        if isinstance(j, int):
            return context[0, 0] if j == n_latent else latent[0, 0, pl.ds(j * tk, tk), :]
        return latent[0, 0, pl.ds(pl.multiple_of(j * tk, tk), tk), :]

    keys = (k_ref, kc_ref)
    values = (v_ref, vc_ref)

    def put_scores(buf, j):
        s_scr[buf, :, pl.ds(0, chunk_len(j))] = lax.dot_general(
            q2_scr[...], kv_chunk(keys, j), (((1,), (1,)), ((), ())), preferred_element_type=F32)

    def softmax_rows(buf, n, part, parts):
        n_blocks = 2 * tq // rb
        for r in range(part * n_blocks // parts, (part + 1) * n_blocks // parts):
            rows = pl.ds(r * rb, rb)
            s = s_scr[buf, rows, pl.ds(0, n)]
            m_prev = m_scr[rows, :]
            m_new = jnp.maximum(m_prev, jnp.max(s, axis=-1, keepdims=True))
            p_scr[buf, rows, pl.ds(0, n)] = jnp.exp2(s - m_new).astype(BF16)
            a_scr[buf, rows, :] = jnp.exp2(m_prev - m_new)
            m_scr[rows, :] = m_new

    def accumulate(buf, j):
        n = chunk_len(j)
        ones = jnp.where(lax.broadcasted_iota(jnp.int32, (n, dv), 1) == 0, 1.0, 0.0).astype(BF16)
        v_ext = jnp.concatenate([kv_chunk(values, j), ones], axis=1)
        acc_scr[...] = a_scr[buf] * acc_scr[...] + _dot(p_scr[buf, :, pl.ds(0, n)], v_ext)

    put_scores(0, 0)
    p_scr[1] = jnp.zeros(p_scr.shape[1:], BF16)
    a_scr[1] = jnp.ones(a_scr.shape[1:], F32)

    def step(j, cur, with_scores):
        prev = max(j - 1, 0) if isinstance(j, int) else jnp.maximum(j - 1, 0)
        n = chunk_len(j)
        softmax_rows(cur, n, 0, 4)
        if with_scores:
            put_scores(1 - cur, j + 1)
        softmax_rows(cur, n, 1, 4)
        softmax_rows(cur, n, 2, 4)
        accumulate(1 - cur, prev)
        softmax_rows(cur, n, 3, 4)

    def body(jj, carry):
        for sub in range(unroll):
            step(unroll * jj + sub, sub % 2, True)
        return carry

    n_loops = (n_latent - 1) // unroll
    lax.fori_loop(0, n_loops, body, 0)
    for j in range(unroll * n_loops, n_chunks):
        step(j, j % 2, j + 1 < n_chunks)
    accumulate((n_chunks - 1) % 2, n_chunks - 1)

    lp = dl_ref[...]
    lam = (jnp.exp(jnp.sum(lp[0:1] * lp[1:2], axis=-1, keepdims=True))
           - jnp.exp(jnp.sum(lp[2:3] * lp[3:4], axis=-1, keepdims=True)) + lam_init)
    o_all = acc_scr[:, :dv] / acc_scr[:, dv:dv + 1]
    o = o_all[:tq] - lam * o_all[tq:]
    y = o * lax.rsqrt(jnp.mean(o * o, axis=-1, keepdims=True) + LN_EPS) * (1.0 - lam_init)
    o_ref[0] = (y * gn_ref[0]).astype(o_ref.dtype)


def _attention(diff_lambda, q, k, v, k_c, v_c, norm_g, *, lam_init):
    bsz, n_heads, seq, dv = q.shape
    n_ctx = k_c.shape[2]
    tq = min(seq, 512)
    tk = min(seq // 2, 512)
    rb = min(2 * tq, 32)
    assert seq % tk == 0 and n_ctx <= tk and n_ctx % LANES == 0
    unroll = 2
    kern = functools.partial(_attn_kernel, tk=tk, rb=rb, unroll=unroll, lam_init=lam_init)
    kv_spec = pl.BlockSpec((1, 1, seq, dv), lambda b, h, i: (b, h, 0, 0))
    ctx_spec = pl.BlockSpec((1, 1, n_ctx, dv), lambda b, h, i: (b, h, 0, 0))
    return pl.pallas_call(
        kern,
        out_shape=jax.ShapeDtypeStruct((bsz, seq, n_heads * dv), BF16),
        grid=(bsz, n_heads, seq // tq),
        in_specs=[pl.BlockSpec(diff_lambda.shape, lambda b, h, i: (0, 0)),
                  pl.BlockSpec((1, 1, tq, dv), lambda b, h, i: (b, h, i, 0)),
                  kv_spec, kv_spec, ctx_spec, ctx_spec,
                  pl.BlockSpec((1, 1, dv), lambda b, h, i: (h, 0, 0))],
        out_specs=pl.BlockSpec((1, tq, dv), lambda b, h, i: (b, i, h)),
        scratch_shapes=[pltpu.VMEM((2 * tq, dv), BF16),
                        pltpu.VMEM((2 * tq, 1), F32),
                        pltpu.VMEM((2 * tq, 2 * dv), F32),
                        pltpu.VMEM((2, 2 * tq, tk), F32),
                        pltpu.VMEM((2, 2 * tq, tk), BF16),
                        pltpu.VMEM((2, 2 * tq, 1), F32)],
        compiler_params=_cparams("parallel", "parallel", "arbitrary"),
        name="diff_attn",
    )(diff_lambda, q, k, v, k_c, v_c, norm_g)


def _mid_kernel(x_ref, yl_ref, ya_ref, wo_ref, g1_ref, sh2_ref, sc2_ref, pg_ref, pb_ref,
                wrh_ref, wrl_ref, br_ref, xmid_ref, h2_ref, route_ref, counts_ref, cnt_scr,
                *, alpha, n_groups, per_group):
    @pl.when((pl.program_id(0) == 0) & (pl.program_id(1) == 0))
    def _():
        cnt_scr[...] = jnp.zeros(cnt_scr.shape, F32)

    lru_w = yl_ref.shape[-1]
    mix = _dot(yl_ref[0], wo_ref[:lru_w, :]) + _dot(ya_ref[0], wo_ref[lru_w:, :])
    x_mid = _layer_norm(alpha * x_ref[0] + g1_ref[0] * mix) * pg_ref[...] + pb_ref[...]
    xmid_ref[0] = x_mid
    h2 = _layer_norm(x_mid) * (1.0 + sc2_ref[0]) + sh2_ref[0]
    h2_ref[0] = h2

    h_hi, h_lo = _split_bf16(h2)
    logits = _dot_split(h_hi, h_lo, wrh_ref[...], wrl_ref[...]) + br_ref[...]
    lane = lax.broadcasted_iota(jnp.int32, logits.shape, 1)

    def first_argmax(vals, vmax):
        return jnp.min(jnp.where(vals == vmax, lane, LANES), axis=-1, keepdims=True)

    gl = jnp.where(lane < n_groups, logits, NEG_BIG)
    g_max = jnp.max(gl, axis=-1, keepdims=True)
    p_g = 1.0 / jnp.sum(jnp.exp(gl - g_max), axis=-1, keepdims=True)
    g_sel = first_argmax(gl, g_max)
    e_lo = n_groups + g_sel * per_group
    el = jnp.where((lane >= e_lo) & (lane < e_lo + per_group), logits, NEG_BIG)
    m1 = jnp.max(el, axis=-1, keepdims=True)
    i1 = first_argmax(el, m1)
    el2 = jnp.where(lane == i1, NEG_BIG, el)
    m2 = jnp.max(el2, axis=-1, keepdims=True)
    i2 = first_argmax(el2, m2)
    e2 = jnp.exp(m2 - m1)
    w1 = 1.0 / (1.0 + e2)
    w2 = e2 * w1

    ex1 = i1 - n_groups
    ex2 = i2 - n_groups
    oh1 = jnp.where(lane == ex1, 1.0, 0.0)
    oh2 = jnp.where(lane == ex2, 1.0, 0.0)
    both = oh1 + oh2
    tm = both.shape[0]
    tri = (lax.broadcasted_iota(jnp.int32, (tm, tm), 0)
           > lax.broadcasted_iota(jnp.int32, (tm, tm), 1)).astype(BF16)
    before = _dot(tri, both.astype(BF16)) + cnt_scr[...]
    rank1 = jnp.sum(oh1 * before, axis=-1, keepdims=True)
    rank2 = jnp.sum(oh2 * before, axis=-1, keepdims=True)
    cnt_scr[...] += jnp.sum(both, axis=0, keepdims=True)
    counts_ref[...] = cnt_scr[...]

    fields = (ex1.astype(F32), ex2.astype(F32), p_g * w1, p_g * w2, rank1, rank2)
    slab = jnp.zeros(logits.shape, F32)
    for idx, val in enumerate(fields):
        slab = jnp.where(lane == idx, val, slab)
    route_ref[0] = slab[:, :ROUTE_W]


def _mid(x, y_lru, y_att, w_out, g1, sh2, sc2, post_g, post_b, wr_hi, wr_lo, b_r,
         *, alpha, n_groups, per_group):
    bsz, seq, d = x.shape
    lru_w = y_lru.shape[-1]
    att_w = y_att.shape[-1]
    tm = min(seq, 512)
    kern = functools.partial(_mid_kernel, alpha=alpha, n_groups=n_groups, per_group=per_group)
    tok = lambda w: pl.BlockSpec((1, tm, w), lambda b, i: (b, i, 0))
    vec = pl.BlockSpec((1, 1, d), lambda b, i: (b, 0, 0))
    const = lambda r, c: pl.BlockSpec((r, c), lambda b, i: (0, 0))
    return pl.pallas_call(
        kern,
        out_shape=(jax.ShapeDtypeStruct((bsz, seq, d), F32),
                   jax.ShapeDtypeStruct((bsz, seq, d), F32),
                   jax.ShapeDtypeStruct((bsz, seq, ROUTE_W), F32),
                   jax.ShapeDtypeStruct((1, LANES), F32)),
        grid=(bsz, seq // tm),
        in_specs=[tok(d), tok(lru_w), tok(att_w), const(lru_w + att_w, d), vec, vec, vec,
                  const(1, d), const(1, d), const(d, LANES), const(d, LANES), const(1, LANES)],
        out_specs=(tok(d), tok(d), tok(ROUTE_W), const(1, LANES)),
        scratch_shapes=[pltpu.VMEM((1, LANES), F32)],
        compiler_params=_cparams("arbitrary", "arbitrary"),
        name="mid_router",
    )(x, y_lru, y_att, w_out, g1, sh2, sc2, post_g, post_b, wr_hi, wr_lo, b_r)


def _row_copy(src, src_row, dst, dst_row, sem):
    return pltpu.make_async_copy(src.at[pl.ds(src_row, 1)], dst.at[pl.ds(dst_row, 1)], sem)


def _drain_rows(src, dst, sem, n):
    pltpu.make_async_copy(src.at[pl.ds(0, n)], dst.at[pl.ds(0, n)], sem).wait()


def _dispatch_kernel(te_ref, nu_ref, slot_ref, h_ref, out_hbm, zero_scr, sem, zero_sem,
                     *, tile, tile_e, n_tiles, n_experts):
    def zero_tile(t):
        dst = out_hbm.at[pl.ds(pl.multiple_of(t * tile_e, tile_e), tile_e)]
        return pltpu.make_async_copy(zero_scr, dst, zero_sem)

    def has_tiles(e):
        return te_ref[e] > (te_ref[e - 1] if e > 0 else 0)

    @pl.when(pl.program_id(0) == 0)
    def _():
        zero_scr[...] = jnp.zeros(zero_scr.shape, F32)
        for e in range(n_experts):
            @pl.when(has_tiles(e))
            def _():
                zero_tile(te_ref[e] - 1).start()
        lax.fori_loop(nu_ref[0], n_tiles, lambda t, c: (zero_tile(t).start(), c)[1], 0)
        for e in range(n_experts):
            @pl.when(has_tiles(e))
            def _():
                zero_tile(0).wait()
        lax.fori_loop(nu_ref[0], n_tiles, lambda t, c: (zero_tile(0).wait(), c)[1], 0)

    def issue(t, carry):
        for k in range(2):
            _row_copy(h_ref, t, out_hbm, slot_ref[0, 0, k * tile + t], sem).start()
        return carry

    lax.fori_loop(0, tile, issue, 0, unroll=ISSUE_UNROLL)
    for _ in range(2):
        _drain_rows(h_ref, out_hbm, sem, tile)


def _dispatch(h2, slots, tile_end, n_used, n_tiles, *, tile, tile_e):
    n, d = h2.shape
    kern = functools.partial(_dispatch_kernel, tile=tile, tile_e=tile_e, n_tiles=n_tiles,
                             n_experts=tile_end.shape[0])
    return pl.pallas_call(
        kern,
        out_shape=jax.ShapeDtypeStruct((n_tiles * tile_e, d), F32),
        grid_spec=pltpu.PrefetchScalarGridSpec(
            num_scalar_prefetch=2,
            grid=(n // tile,),
            in_specs=[pl.BlockSpec((1, 1, 2 * tile), lambda i, te, nu: (i, 0, 0),
                                   memory_space=pltpu.SMEM),
                      pl.BlockSpec((tile, d), lambda i, te, nu: (i, 0))],
            out_specs=pl.BlockSpec(memory_space=pl.ANY),
            scratch_shapes=[pltpu.VMEM((tile_e, d), F32), pltpu.SemaphoreType.DMA(()),
                            pltpu.SemaphoreType.DMA(())]),
        compiler_params=_cparams("arbitrary"),
        name="moe_dispatch",
    )(tile_end, n_used, slots, h2)


def _experts_kernel(te_ref, nu_ref, h_ref, w1_ref, w3_ref, w2_ref, y_ref):
    del te_ref
    used = pl.program_id(0) < nu_ref[0]

    @pl.when(used)
    def _():
        h = h_ref[...].astype(BF16)
        a = _dot(h, w1_ref[0].astype(BF16))
        act = a * _sigmoid(a) * _dot(h, w3_ref[0].astype(BF16))
        y_ref[...] = _dot(act.astype(BF16), w2_ref[0].astype(BF16))

    @pl.when(jnp.logical_not(used))
    def _():
        y_ref[...] = jnp.zeros(y_ref.shape, F32)


def _experts(hs, tile_expert, n_used, w1, w3, w2, *, tile):
    n_rows, d = hs.shape
    d_e = w1.shape[2]
    used = lambda i, te, nu: jnp.minimum(i, nu[0] - 1)
    row_spec = pl.BlockSpec((tile, d), lambda i, te, nu: (used(i, te, nu), 0))
    w_spec = lambda r, c: pl.BlockSpec((1, r, c), lambda i, te, nu: (te[used(i, te, nu)], 0, 0))
    return pl.pallas_call(
        _experts_kernel,
        out_shape=jax.ShapeDtypeStruct((n_rows, d), F32),
        grid_spec=pltpu.PrefetchScalarGridSpec(
            num_scalar_prefetch=2,
            grid=(n_rows // tile,),
            in_specs=[row_spec, w_spec(d, d_e), w_spec(d, d_e), w_spec(d_e, d)],
            out_specs=pl.BlockSpec((tile, d), lambda i, te, nu: (i, 0))),
        compiler_params=_cparams("arbitrary"),
        name="moe_experts",
    )(tile_expert, n_used, hs, w1, w3, w2)


def _combine_kernel(slot_ref, next_slot_ref, route_ref, y_hbm, xmid_ref, g2_ref, pg_ref, pb_ref,
                    out_ref, ya_scr, yb_scr, sem, *, alpha, tile, n_steps):
    i = pl.program_id(0)
    cur = i % 2

    def gather(slots, buf):
        def issue(t, carry):
            _row_copy(y_hbm, slots[0, 0, t], ya_scr.at[buf], t, sem.at[buf]).start()
            _row_copy(y_hbm, slots[0, 0, tile + t], yb_scr.at[buf], t, sem.at[buf]).start()
            return carry
        lax.fori_loop(0, tile, issue, 0, unroll=ISSUE_UNROLL)

    @pl.when(i == 0)
    def _():
        gather(slot_ref, 0)

    @pl.when(i + 1 < n_steps)
    def _():
        gather(next_slot_ref, 1 - cur)

    _drain_rows(y_hbm, ya_scr.at[cur], sem.at[cur], tile)
    _drain_rows(y_hbm, yb_scr.at[cur], sem.at[cur], tile)
    route = route_ref[...]
    y = route[:, 2:3] * ya_scr[cur] + route[:, 3:4] * yb_scr[cur]
    z = alpha * xmid_ref[...] + g2_ref[0] * y
    out_ref[...] = _layer_norm(z) * pg_ref[...] + pb_ref[...]


def _combine(ys, slots, route, x_mid, g2, post_g, post_b, *, alpha, seq, tile):
    n, d = x_mid.shape
    n_steps = n // tile
    assert seq % tile == 0
    tok = lambda w: pl.BlockSpec((tile, w), lambda i: (i, 0))
    const = pl.BlockSpec((1, d), lambda i: (0, 0))
    return pl.pallas_call(
        functools.partial(_combine_kernel, alpha=alpha, tile=tile, n_steps=n_steps),
        out_shape=jax.ShapeDtypeStruct((n, d), F32),
        grid=(n_steps,),
        in_specs=[pl.BlockSpec((1, 1, 2 * tile), lambda i: (i, 0, 0), memory_space=pltpu.SMEM),
                  pl.BlockSpec((1, 1, 2 * tile), lambda i: (jnp.minimum(i + 1, n_steps - 1), 0, 0),
                               memory_space=pltpu.SMEM),
                  tok(ROUTE_W),
                  pl.BlockSpec(memory_space=pl.ANY),
                  tok(d),
                  pl.BlockSpec((1, 1, d), lambda i: (i * tile // seq, 0, 0)),
                  const, const],
        out_specs=tok(d),
        scratch_shapes=[pltpu.VMEM((2, tile, d), F32), pltpu.VMEM((2, tile, d), F32),
                        pltpu.SemaphoreType.DMA((2,))],
        compiler_params=_cparams("arbitrary"),
        name="moe_combine",
    )(slots, slots, route, ys, x_mid, g2, post_g, post_b)


def _moe(h2, route, counts, w1, w3, w2, x_mid, g2, post_g, post_b, *, alpha):
    bsz, seq, d = x_mid.shape
    n_experts = w1.shape[0]
    n = bsz * seq
    tile_e = 512
    tile_t = min(seq, 512)
    n_tiles = 2 * n // tile_e + n_experts
    route = route.reshape(n, ROUTE_W)
    route_t = route.T
    expert = route_t[0:2].astype(jnp.int32)
    rank = route_t[4:6].astype(jnp.int32)
    cnt = counts[0, :n_experts].astype(jnp.int32)
    tiles = (cnt + tile_e - 1) // tile_e
    tile_end = jnp.cumsum(tiles)
    first_row = (tile_end - tiles) * tile_e
    onehot = expert[None] == jnp.arange(n_experts, dtype=jnp.int32)[:, None, None]
    slots = jnp.sum(jnp.where(onehot, first_row[:, None, None], 0), axis=0) + rank
    slots = slots.reshape(2, n // tile_t, tile_t).transpose(1, 0, 2).reshape(n // tile_t, 1, 2 * tile_t)
    tile_ids = jnp.arange(n_tiles, dtype=jnp.int32)
    tile_expert = jnp.minimum(jnp.sum((tile_end[None, :] <= tile_ids[:, None]).astype(jnp.int32), axis=1),
                              n_experts - 1)
    n_used = tile_end[-1:].astype(jnp.int32)
    hs = _dispatch(h2.reshape(n, d), slots, tile_end.astype(jnp.int32), n_used, n_tiles,
                   tile=tile_t, tile_e=tile_e)
    ys = _experts(hs, tile_expert, n_used, w1, w3, w2, tile=tile_e)
    out = _combine(ys, slots, route, x_mid.reshape(n, d), g2, post_g, post_b,
                   alpha=alpha, seq=seq, tile=tile_t)
    return out.reshape(bsz, seq, d)


def _block(x, c, ctx, c_ctx, w_mod, b_mod, w_in, conv_w, conv_b, lru_wa, lru_ba, lru_wi, lru_bi,
           lru_lambda, diff_lambda, attn_norm_g, w_out, post_g, post_b, router_g_w, router_g_b,
           router_e_w, router_e_b, exp_w1, exp_w3, exp_w2, *, grid_w):
    depth = w_mod.shape[0]
    assert depth == 1, "single-layer block only"
    bsz, seq, d = x.shape
    lru_w = conv_w.shape[-1]
    n_heads, dv = attn_norm_g.shape[1:]
    assert dv == HEAD_V_DIM and conv_w.shape[1] == CONV_W
    n_groups, _, per_group = router_e_w.shape[1:]
    assert n_groups + n_groups * per_group <= LANES
    alpha = (2.0 * depth) ** 0.25
    lam_init = 0.8 - 0.6 * math.exp(0.0)

    cond = jnp.concatenate([c, c_ctx[None], jnp.zeros((SUBLANES - 1 - bsz % SUBLANES, d), F32)], axis=0)
    mods = _mods(cond, w_mod[0], b_mod[0]).reshape(cond.shape[0], 6, 1, d)
    sh1, sc1, g1, sh2, sc2, g2 = (mods[:bsz, j] for j in range(6))
    sh1c, sc1c = (jnp.broadcast_to(mods[bsz:bsz + 1, j], (bsz, 1, d)) for j in range(2))

    w_in_b = w_in[0].astype(BF16)
    cos, sin = _rope_tables(seq, grid_w)
    u, gt, q, k, v = _inproj(x, sh1, sc1, w_in_b, cos, sin, lru_w=lru_w, n_heads=n_heads, rope=True)
    n_ctx = ctx.shape[1]
    u_c, _, _, k_c, v_c = _inproj(ctx, sh1c, sc1c, w_in_b, cos[:n_ctx], sin[:n_ctx],
                                  lru_w=lru_w, n_heads=n_heads, rope=False)

    zero_state = jnp.zeros((bsz, 1, lru_w), F32)
    cb = conv_b[0].reshape(1, lru_w)

    def gate_params(direction):
        w_gate = jnp.concatenate([_expand_block_diag(lru_wa[0, direction]),
                                  _expand_block_diag(lru_wi[0, direction])], axis=1).astype(BF16)
        b_gate = jnp.concatenate([lru_ba[0, direction], lru_bi[0, direction]]).reshape(1, 2 * lru_w)
        return w_gate, b_gate, lru_lambda[0, direction].reshape(1, lru_w)

    fwd_p, bwd_p = gate_params(0), gate_params(1)
    _, seed_f = _lru(u_c, conv_w[0], cb, *fwd_p, zero_state, reverse=False)
    _, seed_b = _lru(u_c, conv_w[0], cb, *bwd_p, zero_state, reverse=True)
    h_bwd, _ = _lru(u, conv_w[0], cb, *bwd_p, seed_b, reverse=True)
    y_lru, _ = _lru(u, conv_w[0], cb, *fwd_p, seed_f, reverse=False, other=h_bwd, gate=gt)

    y_att = _attention(diff_lambda[0], q, k, v, k_c, v_c, attn_norm_g[0].reshape(n_heads, 1, dv),
                       lam_init=lam_init)

    w_r = jnp.concatenate([router_g_w[0], jnp.moveaxis(router_e_w[0], 0, 1).reshape(d, -1)], axis=1)
    b_r = jnp.concatenate([router_g_b[0], router_e_b[0].reshape(-1)])
    pad = LANES - w_r.shape[1]
    w_r = jnp.pad(w_r, ((0, 0), (0, pad)))
    b_r = jnp.pad(b_r, (0, pad)).reshape(1, LANES)
    wr_hi, wr_lo = _split_bf16(w_r)

    x_mid, h2, route, counts = _mid(x, y_lru, y_att, w_out[0].astype(BF16), g1, sh2, sc2,
                                    post_g[0, 0].reshape(1, d), post_b[0, 0].reshape(1, d),
                                    wr_hi, wr_lo, b_r, alpha=alpha, n_groups=n_groups,
                                    per_group=per_group)
    return _moe(h2, route, counts, exp_w1[0], exp_w3[0], exp_w2[0], x_mid, g2,
                post_g[0, 1].reshape(1, d),
                post_b[0, 1].reshape(1, d), alpha=alpha)


def kernel(x, c, ctx, c_ctx, w_mod, b_mod, w_in, conv_w, conv_b, lru_wa, lru_ba, lru_wi, lru_bi, lru_lambda, diff_lambda, attn_norm_g, w_out, post_g, post_b, router_g_w, router_g_b, router_e_w, router_e_b, exp_w1, exp_w3, exp_w2):
    return _block(x, c, ctx, c_ctx, w_mod, b_mod, w_in, conv_w, conv_b, lru_wa, lru_ba, lru_wi,
                  lru_bi, lru_lambda, diff_lambda, attn_norm_g, w_out, post_g, post_b, router_g_w,
                  router_g_b, router_e_w, router_e_b, exp_w1, exp_w3, exp_w2, grid_w=GRID_W)
```
